```python
import jax, jax.numpy as jnp
from jax import lax
import numpy as np

D_MODEL = 2048
BATCH = 8
SEQ = 8192
DEPTH = 4

CHUNK = 64
Q_BLOCK = 128
SB_HEADS = 8
SB_HEAD_DIM = 128
SB_WIDTH = SB_HEADS * SB_HEAD_DIM
SGU_GROUPS = 8
SGU_GROUP_DIM = 128
SGU_WIDTH = SGU_GROUPS * SGU_GROUP_DIM
SGU_LEN = 128
D_FF = 4 * D_MODEL
IN_COLS = 3 * SB_WIDTH + 2 * SGU_WIDTH + 2 * D_MODEL
EPS = 1e-6

kernel_name = "hybrid_stickbreak_sgu_block"


def rms_norm(x, g):
    xf = x.astype(jnp.float32)
    y = xf * lax.rsqrt(jnp.mean(xf * xf, axis=-1, keepdims=True) + EPS)
    return (y * g.astype(jnp.float32)).astype(x.dtype)


def layer_norm(x, g, b):
    xf = x.astype(jnp.float32)
    mu = jnp.mean(xf, axis=-1, keepdims=True)
    xc = xf - mu
    y = xc * lax.rsqrt(jnp.mean(xc * xc, axis=-1, keepdims=True) + EPS)
    return (y * g.astype(jnp.float32) + b.astype(jnp.float32)).astype(x.dtype)


def stick_breaking_attention(q, k, v):
    seq = q.shape[2]
    scale = SB_HEAD_DIM ** -0.5
    outs = []
    for blk in range(seq // Q_BLOCK):
        q0 = blk * Q_BLOCK
        kend = q0 + Q_BLOCK
        qb = q[:, :, q0:kend].astype(jnp.float32)
        kb = k[:, :, :kend].astype(jnp.float32)
        vb = v[:, :, :kend]
        z = jnp.einsum('bhtd,bhsd->bhts', qb, kb) * scale
        t_idx = q0 + jnp.arange(Q_BLOCK)[:, None]
        s_idx = jnp.arange(kend)[None, :]
        past = s_idx < t_idx
        log_keep = jnp.where(past, jax.nn.log_sigmoid(-z), 0.0)
        tail = lax.cumsum(log_keep, axis=3, reverse=True) - log_keep
        log_a = jax.nn.log_sigmoid(z) + tail
        a = jnp.where(past, jnp.exp(log_a), 0.0)
        outs.append(jnp.einsum('bhts,bhsd->bhtd', a.astype(v.dtype), vb))
    return jnp.concatenate(outs, axis=2)


def spatial_gating(u, v, ln_g, ln_b, w_s, b_s):
    bsz, seq, _ = v.shape
    v = layer_norm(v, ln_g, ln_b)
    vc = v.reshape(bsz, seq // SGU_LEN, SGU_LEN, SGU_GROUPS, SGU_GROUP_DIM)
    pos = jnp.arange(SGU_LEN)
    mask = (pos[None, :] // CHUNK) <= (pos[:, None] // CHUNK)
    w = jnp.where(mask[None], w_s, jnp.zeros_like(w_s))
    mixed = jnp.einsum('gij,bcjgd->bcigd', w, vc) + b_s.T[:, :, None]
    return u * mixed.reshape(bsz, seq, SGU_WIDTH)


def _fwd_setup_inputs(seed: int = 0) -> dict:
    key = jax.random.key(seed)
    ks = jax.random.split(key, 16)
    f32 = jnp.float32
    nrm = lambda k, shape, s: jax.random.normal(k, shape, f32) * s
    return {
        "x": nrm(ks[0], (BATCH, SEQ, D_MODEL), 1.0),
        "g_mix": 1.0 + nrm(ks[1], (DEPTH, D_MODEL), 0.05),
        "w_in": nrm(ks[2], (DEPTH, D_MODEL, IN_COLS), D_MODEL ** -0.5),
        "g_q": 1.0 + nrm(ks[3], (DEPTH, SB_HEADS, SB_HEAD_DIM), 0.05),
        "g_k": 1.0 + nrm(ks[4], (DEPTH, SB_HEADS, SB_HEAD_DIM), 0.05),
        "sgu_ln_g": 1.0 + nrm(ks[5], (DEPTH, SGU_WIDTH), 0.05),
        "sgu_ln_b": nrm(ks[6], (DEPTH, SGU_WIDTH), 0.02),
        "w_spatial": nrm(ks[7], (DEPTH, SGU_GROUPS, SGU_LEN, SGU_LEN), SGU_LEN ** -0.5),
        "b_spatial": 1.0 + nrm(ks[8], (DEPTH, SGU_GROUPS, SGU_LEN), 0.05),
        "w_oa": nrm(ks[9], (DEPTH, SB_WIDTH, D_MODEL), SB_WIDTH ** -0.5),
        "w_ob": nrm(ks[10], (DEPTH, SGU_WIDTH, D_MODEL), SGU_WIDTH ** -0.5),
        "w_out": nrm(ks[11], (DEPTH, D_MODEL, D_MODEL), D_MODEL ** -0.5),
        "g_ff": 1.0 + nrm(ks[12], (DEPTH, D_MODEL), 0.05),
        "w_ff1": nrm(ks[13], (DEPTH, D_MODEL, D_FF), D_MODEL ** -0.5),
        "w_ff2": nrm(ks[14], (DEPTH, D_FF, D_MODEL), D_FF ** -0.5),
    }


def _fwd_reference(x, g_mix, w_in, g_q, g_k, sgu_ln_g, sgu_ln_b, w_spatial, b_spatial,
              w_oa, w_ob, w_out, g_ff, w_ff1, w_ff2):
    bsz, seq, _ = x.shape
    splits = [SB_WIDTH, 2 * SB_WIDTH, 3 * SB_WIDTH,
              3 * SB_WIDTH + SGU_WIDTH, 3 * SB_WIDTH + 2 * SGU_WIDTH,
              3 * SB_WIDTH + 2 * SGU_WIDTH + D_MODEL]
    for l in range(DEPTH):
        h = rms_norm(x, g_mix[l])
        proj = h @ w_in[l]
        q, k, v_sb, u, v_sg, gate_a, gate_b = jnp.split(proj, splits, axis=-1)

        q = rms_norm(q.reshape(bsz, seq, SB_HEADS, SB_HEAD_DIM), g_q[l])
        k = rms_norm(k.reshape(bsz, seq, SB_HEADS, SB_HEAD_DIM), g_k[l])
        v_sb = v_sb.reshape(bsz, seq, SB_HEADS, SB_HEAD_DIM)
        o = stick_breaking_attention(q.transpose(0, 2, 1, 3), k.transpose(0, 2, 1, 3),
                                     v_sb.transpose(0, 2, 1, 3))
        y_a = o.transpose(0, 2, 1, 3).reshape(bsz, seq, SB_WIDTH) @ w_oa[l]

        u = jax.nn.gelu(u, approximate=False)
        v_sg = jax.nn.gelu(v_sg, approximate=False)
        s = spatial_gating(u, v_sg, sgu_ln_g[l], sgu_ln_b[l], w_spatial[l], b_spatial[l])
        y_b = s @ w_ob[l]

        merged = jax.nn.sigmoid(gate_a) * y_a + jax.nn.sigmoid(gate_b) * y_b
        x = x + merged @ w_out[l]

        h2 = rms_norm(x, g_ff[l])
        x = x + jnp.square(jax.nn.relu(h2 @ w_ff1[l])) @ w_ff2[l]
    return x


import jax as _jax
import jax.numpy as _jnp

TWIN_FORMAT = 'train_step'
FWD_PARAMS = ['x', 'g_mix', 'w_in', 'g_q', 'g_k', 'sgu_ln_g', 'sgu_ln_b', 'w_spatial', 'b_spatial', 'w_oa', 'w_ob', 'w_out', 'g_ff', 'w_ff1', 'w_ff2']
TWIN_WEIGHTS = ['g_mix', 'w_in', 'g_q', 'g_k', 'sgu_ln_g', 'sgu_ln_b', 'w_spatial', 'b_spatial', 'w_oa', 'w_ob', 'w_out', 'g_ff', 'w_ff1', 'w_ff2']
TWIN_DIFF_INPUT = 'x'
TWIN_INPUTS = ['x', 'g_mix', 'w_in', 'g_q', 'g_k', 'sgu_ln_g', 'sgu_ln_b', 'w_spatial', 'b_spatial', 'w_oa', 'w_ob', 'w_out', 'g_ff', 'w_ff1', 'w_ff2', 'loss_target', 'm_g_mix', 'm_w_in', 'm_g_q', 'm_g_k', 'm_sgu_ln_g', 'm_sgu_ln_b', 'm_w_spatial', 'm_b_spatial', 'm_w_oa', 'm_w_ob', 'm_w_out', 'm_g_ff', 'm_w_ff1', 'm_w_ff2', 'v_g_mix', 'v_w_in', 'v_g_q', 'v_g_k', 'v_sgu_ln_g', 'v_sgu_ln_b', 'v_w_spatial', 'v_b_spatial', 'v_w_oa', 'v_w_ob', 'v_w_out', 'v_g_ff', 'v_w_ff1', 'v_w_ff2']
TWIN_OUTPUTS = ['loss', 'grad_x', 'grad_g_mix', 'grad_w_in', 'grad_g_q', 'grad_g_k', 'grad_sgu_ln_g', 'grad_sgu_ln_b', 'grad_w_spatial', 'grad_b_spatial', 'grad_w_oa', 'grad_w_ob', 'grad_w_out', 'grad_g_ff', 'grad_w_ff1', 'grad_w_ff2', 'delta_g_mix', 'delta_w_in', 'delta_g_q', 'delta_g_k', 'delta_sgu_ln_g', 'delta_sgu_ln_b', 'delta_w_spatial', 'delta_b_spatial', 'delta_w_oa', 'delta_w_ob', 'delta_w_out', 'delta_g_ff', 'delta_w_ff1', 'delta_w_ff2', 'new_m_g_mix', 'new_m_w_in', 'new_m_g_q', 'new_m_g_k', 'new_m_sgu_ln_g', 'new_m_sgu_ln_b', 'new_m_w_spatial', 'new_m_b_spatial', 'new_m_w_oa', 'new_m_w_ob', 'new_m_w_out', 'new_m_g_ff', 'new_m_w_ff1', 'new_m_w_ff2', 'new_v_g_mix', 'new_v_w_in', 'new_v_g_q', 'new_v_g_k', 'new_v_sgu_ln_g', 'new_v_sgu_ln_b', 'new_v_w_spatial', 'new_v_b_spatial', 'new_v_w_oa', 'new_v_w_ob', 'new_v_w_out', 'new_v_g_ff', 'new_v_w_ff1', 'new_v_w_ff2']
TWIN_LEAF_KINDS = {'loss': 'loss', 'grad_x': 'grad_x', 'grad_g_mix': 'grad_w', 'grad_w_in': 'grad_w', 'grad_g_q': 'grad_w', 'grad_g_k': 'grad_w', 'grad_sgu_ln_g': 'grad_w', 'grad_sgu_ln_b': 'grad_w', 'grad_w_spatial': 'grad_w', 'grad_b_spatial': 'grad_w', 'grad_w_oa': 'grad_w', 'grad_w_ob': 'grad_w', 'grad_w_out': 'grad_w', 'grad_g_ff': 'grad_w', 'grad_w_ff1': 'grad_w', 'grad_w_ff2': 'grad_w', 'delta_g_mix': 'delta_w', 'delta_w_in': 'delta_w', 'delta_g_q': 'delta_w', 'delta_g_k': 'delta_w', 'delta_sgu_ln_g': 'delta_w', 'delta_sgu_ln_b': 'delta_w', 'delta_w_spatial': 'delta_w', 'delta_b_spatial': 'delta_w', 'delta_w_oa': 'delta_w', 'delta_w_ob': 'delta_w', 'delta_w_out': 'delta_w', 'delta_g_ff': 'delta_w', 'delta_w_ff1': 'delta_w', 'delta_w_ff2': 'delta_w', 'new_m_g_mix': 'new_m', 'new_m_w_in': 'new_m', 'new_m_g_q': 'new_m', 'new_m_g_k': 'new_m', 'new_m_sgu_ln_g': 'new_m', 'new_m_sgu_ln_b': 'new_m', 'new_m_w_spatial': 'new_m', 'new_m_b_spatial': 'new_m', 'new_m_w_oa': 'new_m', 'new_m_w_ob': 'new_m', 'new_m_w_out': 'new_m', 'new_m_g_ff': 'new_m', 'new_m_w_ff1': 'new_m', 'new_m_w_ff2': 'new_m', 'new_v_g_mix': 'new_v', 'new_v_w_in': 'new_v', 'new_v_g_q': 'new_v', 'new_v_g_k': 'new_v', 'new_v_sgu_ln_g': 'new_v', 'new_v_sgu_ln_b': 'new_v', 'new_v_w_spatial': 'new_v', 'new_v_b_spatial': 'new_v', 'new_v_w_oa': 'new_v', 'new_v_w_ob': 'new_v', 'new_v_w_out': 'new_v', 'new_v_g_ff': 'new_v', 'new_v_w_ff1': 'new_v', 'new_v_w_ff2': 'new_v'}


def _forward(args):
    return _fwd_reference(*[args[k] for k in FWD_PARAMS])


def _output_shape():
    def fwd():
        inp = _fwd_setup_inputs(0)
        return _fwd_reference(*[inp[k] for k in FWD_PARAMS])
    out = _jax.eval_shape(fwd)
    return out.shape, out.dtype

N_MICROBATCH = 1
ADAM_LR = 0.001
ADAM_B1 = 0.9
ADAM_B2 = 0.999
ADAM_EPS = 1e-08
ADAM_WD = 0.01
ADAM_STEP = 10
PER_EXAMPLE_BATCH_AXIS = {'x': 0, 'loss_target': 0}
SHARED_INPUTS = []
_WEIGHT_DTYPES = {'g_mix': _jnp.float32, 'w_in': _jnp.float32, 'g_q': _jnp.float32, 'g_k': _jnp.float32, 'sgu_ln_g': _jnp.float32, 'sgu_ln_b': _jnp.float32, 'w_spatial': _jnp.float32, 'b_spatial': _jnp.float32, 'w_oa': _jnp.float32, 'w_ob': _jnp.float32, 'w_out': _jnp.float32, 'g_ff': _jnp.float32, 'w_ff1': _jnp.float32, 'w_ff2': _jnp.float32}
MOMENT_SCALE = {'g_mix': 1.768614e+01, 'w_in': 4.852395e+00, 'g_q': 1.964893e+00, 'g_k': 1.977280e+00, 'sgu_ln_g': 7.174519e+00, 'sgu_ln_b': 2.224890e+00, 'w_spatial': 6.596470e-01, 'b_spatial': 8.118783e+00, 'w_oa': 8.527707e+00, 'w_ob': 1.002678e+01, 'w_out': 1.315648e+01, 'g_ff': 1.007373e+02, 'w_ff1': 8.806613e+00, 'w_ff2': 3.024230e+01}


def _to_microbatches(a, axis):
    t = _jnp.moveaxis(a, axis, 0)
    t = t.reshape((N_MICROBATCH, t.shape[0] // N_MICROBATCH) + t.shape[1:])
    return _jnp.moveaxis(t, 1, axis + 1)


def setup_inputs(seed: int = 0) -> dict:
    inp = _fwd_setup_inputs(seed)
    key = _jax.random.fold_in(_jax.random.key(seed), 7919)
    shape, _ = _output_shape()
    out = dict(inp)
    out["loss_target"] = _jax.random.normal(_jax.random.fold_in(key, 0), shape, _jnp.float32)
    for i, name in enumerate(TWIN_WEIGHTS):
        w = inp[name].astype(_jnp.float32)
        if MOMENT_SCALE is None:
            s = _jnp.sqrt(_jnp.mean(_jnp.square(w)) + 1e-30)
        else:
            s = MOMENT_SCALE[name]
        km, kv = _jax.random.split(_jax.random.fold_in(key, i + 1))
        out[name] = w
        out["m_" + name] = s * _jax.random.normal(km, w.shape, _jnp.float32)
        out["v_" + name] = (s * s) * _jax.random.uniform(kv, w.shape, _jnp.float32, 0.5, 1.5)
    if N_MICROBATCH > 1:
        for name, axis in PER_EXAMPLE_BATCH_AXIS.items():
            out[name] = _to_microbatches(out[name], axis)
    return {'x': out['x'], 'g_mix': out['g_mix'], 'w_in': out['w_in'], 'g_q': out['g_q'], 'g_k': out['g_k'], 'sgu_ln_g': out['sgu_ln_g'], 'sgu_ln_b': out['sgu_ln_b'], 'w_spatial': out['w_spatial'], 'b_spatial': out['b_spatial'], 'w_oa': out['w_oa'], 'w_ob': out['w_ob'], 'w_out': out['w_out'], 'g_ff': out['g_ff'], 'w_ff1': out['w_ff1'], 'w_ff2': out['w_ff2'], 'loss_target': out['loss_target'], 'm_g_mix': out['m_g_mix'], 'm_w_in': out['m_w_in'], 'm_g_q': out['m_g_q'], 'm_g_k': out['m_g_k'], 'm_sgu_ln_g': out['m_sgu_ln_g'], 'm_sgu_ln_b': out['m_sgu_ln_b'], 'm_w_spatial': out['m_w_spatial'], 'm_b_spatial': out['m_b_spatial'], 'm_w_oa': out['m_w_oa'], 'm_w_ob': out['m_w_ob'], 'm_w_out': out['m_w_out'], 'm_g_ff': out['m_g_ff'], 'm_w_ff1': out['m_w_ff1'], 'm_w_ff2': out['m_w_ff2'], 'v_g_mix': out['v_g_mix'], 'v_w_in': out['v_w_in'], 'v_g_q': out['v_g_q'], 'v_g_k': out['v_g_k'], 'v_sgu_ln_g': out['v_sgu_ln_g'], 'v_sgu_ln_b': out['v_sgu_ln_b'], 'v_w_spatial': out['v_w_spatial'], 'v_b_spatial': out['v_b_spatial'], 'v_w_oa': out['v_w_oa'], 'v_w_ob': out['v_w_ob'], 'v_w_out': out['v_w_out'], 'v_g_ff': out['v_g_ff'], 'v_w_ff1': out['v_w_ff1'], 'v_w_ff2': out['v_w_ff2']}


def _loss(weights, diff, rest, loss_target):
    with _jax.named_scope("forward"):
        args = {**rest, TWIN_DIFF_INPUT: diff, **{k: w.astype(_WEIGHT_DTYPES[k]) for k, w in weights.items()}}
        y = _forward(args)
    with _jax.named_scope("loss_head"):
        err = _jnp.square(y.astype(_jnp.float32) - loss_target)
        return 0.5 * _jnp.sum(_jnp.mean(err, axis=-1)) if err.ndim else 0.5 * err


def _adamw(w, g, m, v):
    m = ADAM_B1 * m + (1.0 - ADAM_B1) * g
    v = ADAM_B2 * v + (1.0 - ADAM_B2) * _jnp.square(g)
    m_hat = m / (1.0 - ADAM_B1 ** ADAM_STEP)
    v_hat = v / (1.0 - ADAM_B2 ** ADAM_STEP)
    delta = -ADAM_LR * (m_hat / (_jnp.sqrt(v_hat) + ADAM_EPS) + ADAM_WD * w)
    return delta, m, v


def reference(x, g_mix, w_in, g_q, g_k, sgu_ln_g, sgu_ln_b, w_spatial, b_spatial, w_oa, w_ob, w_out, g_ff, w_ff1, w_ff2, loss_target, m_g_mix, m_w_in, m_g_q, m_g_k, m_sgu_ln_g, m_sgu_ln_b, m_w_spatial, m_b_spatial, m_w_oa, m_w_ob, m_w_out, m_g_ff, m_w_ff1, m_w_ff2, v_g_mix, v_w_in, v_g_q, v_g_k, v_sgu_ln_g, v_sgu_ln_b, v_w_spatial, v_b_spatial, v_w_oa, v_w_ob, v_w_out, v_g_ff, v_w_ff1, v_w_ff2):
    given = dict(x=x, g_mix=g_mix, w_in=w_in, g_q=g_q, g_k=g_k, sgu_ln_g=sgu_ln_g, sgu_ln_b=sgu_ln_b, w_spatial=w_spatial, b_spatial=b_spatial, w_oa=w_oa, w_ob=w_ob, w_out=w_out, g_ff=g_ff, w_ff1=w_ff1, w_ff2=w_ff2, loss_target=loss_target, m_g_mix=m_g_mix, m_w_in=m_w_in, m_g_q=m_g_q, m_g_k=m_g_k, m_sgu_ln_g=m_sgu_ln_g, m_sgu_ln_b=m_sgu_ln_b, m_w_spatial=m_w_spatial, m_b_spatial=m_b_spatial, m_w_oa=m_w_oa, m_w_ob=m_w_ob, m_w_out=m_w_out, m_g_ff=m_g_ff, m_w_ff1=m_w_ff1, m_w_ff2=m_w_ff2, v_g_mix=v_g_mix, v_w_in=v_w_in, v_g_q=v_g_q, v_g_k=v_g_k, v_sgu_ln_g=v_sgu_ln_g, v_sgu_ln_b=v_sgu_ln_b, v_w_spatial=v_w_spatial, v_b_spatial=v_b_spatial, v_w_oa=v_w_oa, v_w_ob=v_w_ob, v_w_out=v_w_out, v_g_ff=v_g_ff, v_w_ff1=v_w_ff1, v_w_ff2=v_w_ff2)
    weights = {n: given[n] for n in TWIN_WEIGHTS}
    shared = {n: given[n] for n in SHARED_INPUTS}
    per_example = {n: given[n] for n in ['x']}
    grad_fn = _jax.value_and_grad(_loss, argnums=(0, 1))

    def one_microbatch(ex, loss_target):
        ex = dict(ex)
        diff = ex.pop(TWIN_DIFF_INPUT)
        return grad_fn(weights, diff, {**shared, **ex}, loss_target)

    if N_MICROBATCH == 1:
        loss, (grad_w, grad_x) = one_microbatch(per_example, given["loss_target"])
    else:
        def body(carry, xs):
            loss_sum, grad_sum = carry
            l_k, (gw_k, gx_k) = one_microbatch(xs[0], xs[1])
            with _jax.named_scope("update"):
                return (loss_sum + l_k, _jax.tree.map(_jnp.add, grad_sum, gw_k)), gx_k

        init = (_jnp.zeros((), _jnp.float32), _jax.tree.map(_jnp.zeros_like, weights))
        (loss, grad_w), grad_x = _jax.lax.scan(body, init, (per_example, given["loss_target"]))
    with _jax.named_scope("update"):
        delta_w, new_m, new_v = {}, {}, {}
        for n in TWIN_WEIGHTS:
            delta_w[n], new_m[n], new_v[n] = _adamw(weights[n], grad_w[n], given["m_" + n], given["v_" + n])
    return (loss, grad_x, *[grad_w[n] for n in TWIN_WEIGHTS], *[delta_w[n] for n in TWIN_WEIGHTS],
            *[new_m[n] for n in TWIN_WEIGHTS], *[new_v[n] for n in TWIN_WEIGHTS])
```

```python
import functools
import math

import jax
import jax.numpy as jnp
from jax import lax
from jax.experimental import pallas as pl
from jax.experimental.pallas import tpu as pltpu

F32 = jnp.float32
BF16 = jnp.bfloat16
SDS = jax.ShapeDtypeStruct
MESH = pl.DeviceIdType.MESH

N_DEV = 8
HEAD_DIM = 128
GROUP_DIM = 128
SGU_LEN = 128
SGU_CAUSAL = 64
NORM_EPS = 1e-6
ATT_BLOCK = 128
ATT_DEAD = -110.0

ADAM_LR, ADAM_B1, ADAM_B2, ADAM_EPS, ADAM_WD, ADAM_STEP = 0.001, 0.9, 0.999, 1e-08, 0.01, 10

ROW_TILE = 512
MM_TM, MM_TN, MM_TK = 1024, 1152, 2048


def _tile(n, target, align=128):
    best = None
    for t in range(align, min(n, target) + 1, align):
        if n % t == 0:
            best = t
    return n if best is None else best


def _call(name, grid, ins, outs, body, scratch=(), semantics=None):
    return pl.pallas_call(
        body, name=name, grid=grid,
        in_specs=[pl.BlockSpec(blk, im) for _, blk, im in ins],
        out_specs=[pl.BlockSpec(blk, im) for _, blk, im in outs],
        out_shape=[s for s, _, _ in outs],
        scratch_shapes=list(scratch),
        compiler_params=pltpu.CompilerParams(dimension_semantics=semantics or ("arbitrary",) * len(grid)),
    )(*[a for a, _, _ in ins])


def _dot(a, b, dims=((1,), (0,))):
    return lax.dot_general(a, b, (dims, ((), ())), preferred_element_type=F32)


NN = ((1,), (0,))
NT = ((1,), (1,))
TN = ((0,), (0,))


def _matmul(name, grid, a, b, dims, outs, epilogue=None, extras=()):
    nk = grid[2]
    n_ex, n_out = len(extras), len(outs)
    acc_shape = tuple(d for d in outs[0][1] if d is not None)

    def body(*refs):
        a_ref, b_ref = refs[0], refs[1]
        ex_refs = refs[2:2 + n_ex]
        out_refs = refs[2 + n_ex:2 + n_ex + n_out]
        part = _dot(a_ref[...].astype(BF16), b_ref[...].astype(BF16), dims)

        def finish(acc):
            vals = (acc,) if epilogue is None else epilogue(acc, *[r[...] for r in ex_refs])
            for r, v in zip(out_refs, vals):
                r[...] = v.astype(r.dtype)

        if nk == 1:
            finish(part)
        else:
            acc_ref = refs[-1]
            k = pl.program_id(2)

            @pl.when(k == 0)
            def _():
                acc_ref[...] = part

            @pl.when(k > 0)
            def _():
                acc_ref[...] += part

            @pl.when(k == nk - 1)
            def _():
                finish(acc_ref[...])

    return _call(name, grid, [a, b, *extras], list(outs), body,
                 scratch=[pltpu.VMEM(acc_shape, F32)] if nk > 1 else [],
                 semantics=("parallel", "parallel", "arbitrary"))


def _w_cols(w, l, tk, tn):
    nps = w.shape[3] // tn
    return (w, (None, None, tk, tn), lambda i, j, k: (j // nps, l, k, j % nps))


def _w_rows(w, l, tk, tn):
    kps = w.shape[2] // tk
    return (w, (None, None, tk, tn), lambda i, j, k: (k // kps, l, k % kps, j))


def _w_cols_t(w, l, tn, tk):
    kps = w.shape[3] // tk
    return (w, (None, None, tn, tk), lambda i, j, k: (k // kps, l, j, k % kps))


def _w_rows_t(w, l, tn, tk):
    nps = w.shape[2] // tn
    return (w, (None, None, tn, tk), lambda i, j, k: (j // nps, l, j % nps, k))


def _mm_fwd(name, a, wspec_fn, w, l, n_out, k_dim, outs_fn, epilogue=None, extras_fn=None, tn_div=None, k_div=None):
    m = a.shape[0]
    tm = _tile(m, MM_TM, 8)
    shard_n = tn_div if tn_div is not None else n_out
    tn = _tile(shard_n, MM_TN)
    tk = _tile(k_div if k_div is not None else k_dim, MM_TK)
    grid = (m // tm, n_out // tn, k_dim // tk)
    extras = extras_fn(tm, tn) if extras_fn else ()
    return _matmul(name, grid, (a, (tm, tk), lambda i, j, k: (i, k)), wspec_fn(w, l, tk, tn), NN,
                   outs_fn(tm, tn), epilogue, extras)


def _mm_bwd_x(name, a, wspec_fn, w, l, n_out, k_dim, k_div, outs_fn, epilogue=None, extras_fn=None, tn_div=None):
    m = a.shape[0]
    tm = _tile(m, MM_TM, 8)
    tn = _tile(tn_div if tn_div is not None else n_out, 1024)
    tk = _tile(k_div, MM_TK)
    grid = (m // tm, n_out // tn, k_dim // tk)
    extras = extras_fn(tm, tn) if extras_fn else ()
    return _matmul(name, grid, (a, (tm, tk), lambda i, j, k: (i, k)), wspec_fn(w, l, tn, tk), NT,
                   outs_fn(tm, tn), epilogue, extras)


def _mm_bwd_w(name, a, b, col_sharded):
    m, ka = a.shape
    n = b.shape[1]
    ts = _tile(m, 1024, 8)
    if col_sharded:
        shard = n // N_DEV
        ti, tj = _tile(ka, 1024), _tile(shard, MM_TN)
        nps = shard // tj
        out = (SDS((N_DEV, ka, shard), BF16), (None, ti, tj), lambda i, j, k: (j // nps, i, j % nps))
    else:
        shard = ka // N_DEV
        ti, tj = _tile(shard, 1024), _tile(n, 1024)
        ips = shard // ti
        out = (SDS((N_DEV, shard, n), BF16), (None, ti, tj), lambda i, j, k: (i // ips, i % ips, j))
    grid = (ka // ti, n // tj, m // ts)
    return _matmul(name, grid, (a, (ts, ti), lambda i, j, k: (k, i)), (b, (ts, tj), lambda i, j, k: (k, j)), TN, [out])[0]


def _rms_fwd(name, x, g3, l):
    s, d = x.shape
    ts = _tile(s, ROW_TILE, 8)

    def body(x_ref, g_ref, h_ref):
        xv = x_ref[...]
        r = lax.rsqrt(jnp.mean(xv * xv, axis=-1, keepdims=True) + NORM_EPS)
        h_ref[...] = (xv * r * g_ref[...]).astype(BF16)

    return _call(name, (s // ts,), [(x, (ts, d), lambda i: (i, 0)), (g3, (None, 1, d), lambda i: (l, 0, 0))],
                 [(SDS((s, d), BF16), (ts, d), lambda i: (i, 0))], body, semantics=("parallel",))[0]


def _rms_bwd(name, x, g3, l, dh, dres):
    s, d = x.shape
    ts = _tile(s, ROW_TILE, 8)

    def body(x_ref, g_ref, dh_ref, dres_ref, dx_ref, dxb_ref, dg_ref):
        xv = x_ref[...]
        r = lax.rsqrt(jnp.mean(xv * xv, axis=-1, keepdims=True) + NORM_EPS)
        xhat = xv * r
        dhv = dh_ref[...]
        gy = dhv * g_ref[...]
        m = jnp.mean(gy * xhat, axis=-1, keepdims=True)
        dx = dres_ref[...] + r * (gy - xhat * m)
        dx_ref[...] = dx
        dxb_ref[...] = dx.astype(BF16)
        part = jnp.sum(dhv * xhat, axis=0, keepdims=True)

        @pl.when(pl.program_id(0) == 0)
        def _():
            dg_ref[...] = part

        @pl.when(pl.program_id(0) > 0)
        def _():
            dg_ref[...] += part

    row = lambda i: (i, 0)
    return _call(name, (s // ts,),
                 [(x, (ts, d), row), (g3, (None, 1, d), lambda i: (l, 0, 0)), (dh, (ts, d), row), (dres, (ts, d), row)],
                 [(SDS((s, d), F32), (ts, d), row), (SDS((s, d), BF16), (ts, d), row),
                  (SDS((1, d), F32), (1, d), lambda i: (0, 0))], body)


def _loss_head(y, target):
    s, d = y.shape
    ts = _tile(s, ROW_TILE, 8)

    def body(y_ref, t_ref, dy_ref, dyb_ref, sq_ref):
        diff = y_ref[...] - t_ref[...]
        dy = diff / d
        dy_ref[...] = dy
        dyb_ref[...] = dy.astype(BF16)
        part = jnp.sum(diff * diff, axis=0, keepdims=True)

        @pl.when(pl.program_id(0) == 0)
        def _():
            sq_ref[...] = part

        @pl.when(pl.program_id(0) > 0)
        def _():
            sq_ref[...] += part

    row = lambda i: (i, 0)
    return _call("loss_head", (s // ts,), [(y, (ts, d), row), (target, (ts, d), row)],
                 [(SDS((s, d), F32), (ts, d), row), (SDS((s, d), BF16), (ts, d), row),
                  (SDS((1, d), F32), (1, d), lambda i: (0, 0))], body)


def _qkv_prep(proj, gq3, gk3, l, sbw):
    s = proj.shape[0]
    ts = _tile(s, ROW_TILE, 8)
    heads = sbw // HEAD_DIM

    def body(q_ref, k_ref, v_ref, gq_ref, gk_ref, qn_ref, kn_ref, vb_ref):
        for h in range(heads):
            sl = slice(h * HEAD_DIM, (h + 1) * HEAD_DIM)
            for src, g_ref, dst in ((q_ref, gq_ref, qn_ref), (k_ref, gk_ref, kn_ref)):
                t = src[:, sl]
                r = lax.rsqrt(jnp.mean(t * t, axis=-1, keepdims=True) + NORM_EPS)
                dst[:, sl] = (t * r * g_ref[:, sl]).astype(BF16)
        vb_ref[...] = v_ref[...].astype(BF16)

    gspec = lambda g: (g, (None, 1, sbw), lambda i: (l, 0, 0))
    col = lambda c: (proj, (ts, sbw), lambda i: (i, c))
    out = (SDS((s, sbw), BF16), (ts, sbw), lambda i: (i, 0))
    return _call("qkv_prep", (s // ts,), [col(0), col(1), col(2), gspec(gq3), gspec(gk3)], [out, out, out], body,
                 semantics=("parallel",))


def _qk_bwd(proj, gq3, gk3, l, sbw, dqn, dkn, dv):
    s = proj.shape[0]
    ts = _tile(s, ROW_TILE, 8)
    heads = sbw // HEAD_DIM

    def body(q_ref, k_ref, gq_ref, gk_ref, dqn_ref, dkn_ref, dv_ref, dq_ref, dk_ref, dvb_ref, dgq_ref, dgk_ref):
        first = pl.program_id(0) == 0
        for src, g_ref, dn_ref, dst, dg_ref in ((q_ref, gq_ref, dqn_ref, dq_ref, dgq_ref),
                                                (k_ref, gk_ref, dkn_ref, dk_ref, dgk_ref)):
            for h in range(heads):
                sl = slice(h * HEAD_DIM, (h + 1) * HEAD_DIM)
                t = src[:, sl]
                r = lax.rsqrt(jnp.mean(t * t, axis=-1, keepdims=True) + NORM_EPS)
                xhat = t * r
                dn = dn_ref[:, sl]
                gy = dn * g_ref[:, sl]
                m = jnp.mean(gy * xhat, axis=-1, keepdims=True)
                dst[:, sl] = (r * (gy - xhat * m)).astype(BF16)
                part = jnp.sum(dn * xhat, axis=0, keepdims=True)

                @pl.when(first)
                def _():
                    dg_ref[:, sl] = part

                @pl.when(jnp.logical_not(first))
                def _():
                    dg_ref[:, sl] += part
        dvb_ref[...] = dv_ref[...].astype(BF16)

    gspec = lambda g: (g, (None, 1, sbw), lambda i: (l, 0, 0))
    col = lambda c: (proj, (ts, sbw), lambda i: (i, c))
    row = lambda a: (a, (ts, sbw), lambda i: (i, 0))
    outb = (SDS((s, sbw), BF16), (ts, sbw), lambda i: (i, 0))
    outg = (SDS((1, sbw), F32), (1, sbw), lambda i: (0, 0))
    return _call("qk_bwd", (s // ts,), [col(0), col(1), gspec(gq3), gspec(gk3), row(dqn), row(dkn), row(dv)],
                 [outb, outb, outb, outg, outg], body)


def _softplus(z):
    return jnp.maximum(z, 0.0) + jnp.log1p(jnp.exp(-jnp.abs(z)))


def _tri_dot(x, tri):
    hi = x.astype(BF16)
    lo = (x - hi.astype(F32)).astype(BF16)
    return _dot(hi, tri) + _dot(lo, tri)


def _att_common(tb):
    rows = lax.broadcasted_iota(jnp.int32, (tb, tb), 0)
    cols = lax.broadcasted_iota(jnp.int32, (tb, tb), 1)
    suffix = (rows >= cols).astype(BF16)
    prefix = (rows <= cols).astype(BF16)
    return rows, cols, suffix, prefix


def _key_norm_max(k_ref, kmax_ref):
    kf = k_ref[...].astype(F32)
    n2 = jnp.sum(kf * kf, axis=1, keepdims=True)
    kmax_ref[...] = jnp.broadcast_to(jnp.max(n2, axis=0, keepdims=True), kmax_ref.shape)


def _att_reach(q, kmax_ref, scale):
    qf = q.astype(F32)
    qn2 = jnp.sum(qf * qf, axis=1, keepdims=True)
    return scale * jnp.sqrt(qn2 * kmax_ref[0:1, 0:1]) * 1.001 + 1e-3


def _attn_fwd(qn, kn, vb):
    s, sbw = qn.shape
    heads = sbw // HEAD_DIM
    tb = _tile(s, ATT_BLOCK, 8)
    scale = HEAD_DIM ** -0.5

    def body(q_ref, k_ref, v_ref, o_ref, kmax_ref):
        qi = pl.program_id(1)

        @pl.when(qi == 0)
        def _():
            _key_norm_max(k_ref, kmax_ref)

        rows, cols, suffix, _ = _att_common(tb)
        q = q_ref[...]
        reach = _att_reach(q, kmax_ref, scale)

        def cond(c):
            return jnp.logical_and(c[0] >= 0, c[3])

        def step(c):
            kb, run, acc, _ = c
            off = pl.multiple_of(kb * tb, tb)
            z = _dot(q, k_ref[pl.ds(off, tb), :], NT) * scale
            past = (cols + kb * tb) < (rows + qi * tb)
            keep = jnp.where(past, -_softplus(z), 0.0)
            csum = _tri_dot(keep, suffix)
            a = jnp.where(past, jnp.exp(z + csum + run), 0.0)
            acc = acc + _dot(a.astype(BF16), v_ref[pl.ds(off, tb), :])
            run = run + csum[:, 0:1]
            return kb - 1, run, acc, jnp.max(run + reach) > ATT_DEAD

        init = (qi, jnp.zeros((tb, 1), F32), jnp.zeros((tb, HEAD_DIM), F32), jnp.bool_(True))
        o_ref[...] = lax.while_loop(cond, step, init)[2].astype(o_ref.dtype)

    qspec = lambda a: (a, (tb, HEAD_DIM), lambda h, i: (i, h))
    full = lambda a: (a, (s, HEAD_DIM), lambda h, i: (0, h))
    return _call("attn_fwd", (heads, s // tb), [qspec(qn), full(kn), full(vb)],
                 [(SDS((s, sbw), BF16), (tb, HEAD_DIM), lambda h, i: (i, h))], body,
                 scratch=[pltpu.VMEM((8, 128), F32)], semantics=("parallel", "arbitrary"))[0]


def _attn_bwd(qn, kn, vb, do):
    s, sbw = qn.shape
    heads = sbw // HEAD_DIM
    tb = _tile(s, ATT_BLOCK, 8)
    nb = s // tb
    scale = HEAD_DIM ** -0.5

    def body(q_ref, k_ref, v_ref, do_ref, dq_ref, dk_ref, dv_ref, kmax_ref, g_ref, sg_ref):
        qi = pl.program_id(1)

        @pl.when(qi == 0)
        def _():
            _key_norm_max(k_ref, kmax_ref)
            dk_ref[...] = jnp.zeros_like(dk_ref)
            dv_ref[...] = jnp.zeros_like(dv_ref)

        rows, cols, suffix, prefix = _att_common(tb)
        q = q_ref[...]
        dob = do_ref[...]
        reach = _att_reach(q, kmax_ref, scale)

        def cond(c):
            return jnp.logical_and(c[0] >= 0, c[2])

        def sweep_back(c):
            kb, run, _ = c
            off = pl.multiple_of(kb * tb, tb)
            z = _dot(q, k_ref[pl.ds(off, tb), :], NT) * scale
            past = (cols + kb * tb) < (rows + qi * tb)
            keep = jnp.where(past, -_softplus(z), 0.0)
            csum = _tri_dot(keep, suffix)
            a = jnp.where(past, jnp.exp(z + csum + run), 0.0)
            da = _dot(dob, v_ref[pl.ds(off, tb), :], NT)
            g_ref[kb] = a * da
            sg_ref[kb] = jax.nn.sigmoid(z)
            dv_ref[pl.ds(off, tb), :] += _dot(a.astype(BF16), dob, TN)
            run = run + csum[:, 0:1]
            return kb - 1, run, jnp.max(run + reach) > ATT_DEAD

        stop = lax.while_loop(cond, sweep_back, (qi, jnp.zeros((tb, 1), F32), jnp.bool_(True)))[0]

        def sweep_fwd(kb, c):
            psum, dq = c
            off = pl.multiple_of(kb * tb, tb)
            g = g_ref[kb]
            pin = _tri_dot(g, prefix)
            past = (cols + kb * tb) < (rows + qi * tb)
            dz = jnp.where(past, g - sg_ref[kb] * (psum + pin), 0.0)
            dzb = (dz * scale).astype(BF16)
            dq = dq + _dot(dzb, k_ref[pl.ds(off, tb), :])
            dk_ref[pl.ds(off, tb), :] += _dot(dzb, q, TN)
            return psum + pin[:, tb - 1:tb], dq

        init = (jnp.zeros((tb, 1), F32), jnp.zeros((tb, HEAD_DIM), F32))
        dq_ref[...] = lax.fori_loop(stop + 1, qi + 1, sweep_fwd, init)[1]

    qspec = lambda a: (a, (tb, HEAD_DIM), lambda h, i: (i, h))
    full = lambda a: (a, (s, HEAD_DIM), lambda h, i: (0, h))
    outq = (SDS((s, sbw), F32), (tb, HEAD_DIM), lambda h, i: (i, h))
    outf = (SDS((s, sbw), F32), (s, HEAD_DIM), lambda h, i: (0, h))
    return _call("attn_bwd", (heads, nb), [qspec(qn), full(kn), full(vb), qspec(do)], [outq, outf, outf], body,
                 scratch=[pltpu.VMEM((8, 128), F32), pltpu.VMEM((nb, tb, tb), F32), pltpu.VMEM((nb, tb, tb), F32)],
                 semantics=("parallel", "arbitrary"))


_SQRT_HALF = 0.7071067811865476
_INV_SQRT_2PI = 0.3989422804014327


def _gelu(x):
    return 0.5 * x * (1.0 + lax.erf(x * _SQRT_HALF))


def _gelu_grad(x):
    return 0.5 * (1.0 + lax.erf(x * _SQRT_HALF)) + x * jnp.exp(-0.5 * x * x) * _INV_SQRT_2PI


def _sgu_mask():
    i = lax.broadcasted_iota(jnp.int32, (SGU_LEN, SGU_LEN), 0) // SGU_CAUSAL
    j = lax.broadcasted_iota(jnp.int32, (SGU_LEN, SGU_LEN), 1) // SGU_CAUSAL
    return j <= i


def _sgu_norm(vg, lng, lnb):
    mu = jnp.mean(vg, axis=-1, keepdims=True)
    xc = vg - mu
    rstd = lax.rsqrt(jnp.mean(xc * xc, axis=-1, keepdims=True) + NORM_EPS)
    xhat = xc * rstd
    return xhat, rstd, xhat * lng + lnb


def _sgu_specs(proj, ln_g3, ln_b3, w_sp, b_col, l, sbw, sgw, rb):
    groups = sgw // GROUP_DIM
    ucol, vcol = 3 * sbw // sgw, 3 * sbw // sgw + 1
    vec = lambda a: (a, (None, 1, sgw), lambda i: (l, 0, 0))
    mat = lambda a: (a, (None, groups, SGU_LEN, SGU_LEN), lambda i: (l, 0, 0, 0))
    return [(proj, (rb, sgw), lambda i: (i, ucol)), (proj, (rb, sgw), lambda i: (i, vcol)),
            vec(ln_g3), vec(ln_b3), mat(w_sp), mat(b_col)]


def _sgu_fwd(proj, ln_g3, ln_b3, w_sp, b_col, l, sbw, sgw):
    s = proj.shape[0]
    rb = _tile(s, 2 * SGU_LEN, SGU_LEN)
    groups = sgw // GROUP_DIM

    def body(u_ref, v_ref, lng_ref, lnb_ref, w_ref, b_ref, s_ref):
        mask = _sgu_mask()
        _, _, vln = _sgu_norm(_gelu(v_ref[...]), lng_ref[...], lnb_ref[...])
        vb = vln.astype(BF16)
        for g in range(groups):
            cs = slice(g * GROUP_DIM, (g + 1) * GROUP_DIM)
            wm = jnp.where(mask, w_ref[g], 0.0).astype(BF16)
            for c in range(rb // SGU_LEN):
                rs = slice(c * SGU_LEN, (c + 1) * SGU_LEN)
                mixed = _dot(wm, vb[rs, cs]) + b_ref[g]
                s_ref[rs, cs] = (_gelu(u_ref[rs, cs]) * mixed).astype(BF16)

    return _call("sgu_fwd", (s // rb,), _sgu_specs(proj, ln_g3, ln_b3, w_sp, b_col, l, sbw, sgw, rb),
                 [(SDS((s, sgw), BF16), (rb, sgw), lambda i: (i, 0))], body, semantics=("parallel",))[0]


def _sgu_bwd(proj, ln_g3, ln_b3, w_sp, b_col, l, sbw, sgw, ds):
    s = proj.shape[0]
    rb = _tile(s, 2 * SGU_LEN, SGU_LEN)
    groups = sgw // GROUP_DIM
    nsteps = s // rb

    def body(u_ref, v_ref, lng_ref, lnb_ref, w_ref, b_ref, ds_ref,
             du_ref, dv_ref, dlng_ref, dlnb_ref, dw_ref, db_ref, dvln_ref, dw_acc, db_acc):
        step = pl.program_id(0)

        @pl.when(step == 0)
        def _():
            dw_acc[...] = jnp.zeros_like(dw_acc)
            db_acc[...] = jnp.zeros_like(db_acc)

        mask = _sgu_mask()
        vp = v_ref[...]
        lng = lng_ref[...]
        xhat, rstd, vln = _sgu_norm(_gelu(vp), lng, lnb_ref[...])
        vb = vln.astype(BF16)
        for g in range(groups):
            cs = slice(g * GROUP_DIM, (g + 1) * GROUP_DIM)
            wm = jnp.where(mask, w_ref[g], 0.0).astype(BF16)
            for c in range(rb // SGU_LEN):
                rs = slice(c * SGU_LEN, (c + 1) * SGU_LEN)
                vbp = vb[rs, cs]
                mixed = _dot(wm, vbp) + b_ref[g]
                up = u_ref[rs, cs]
                dsp = ds_ref[rs, cs]
                du_ref[rs, cs] = (dsp * mixed * _gelu_grad(up)).astype(BF16)
                dmixed = dsp * _gelu(up)
                dmb = dmixed.astype(BF16)
                dvln_ref[rs, cs] = _dot(wm, dmb, TN)
                dw_acc[g] += _dot(dmb, vbp, NT)
                db_acc[g] += dmixed
        dvln = dvln_ref[...]
        part_g = jnp.sum(dvln * xhat, axis=0, keepdims=True)
        part_b = jnp.sum(dvln, axis=0, keepdims=True)

        @pl.when(step == 0)
        def _():
            dlng_ref[...] = part_g
            dlnb_ref[...] = part_b

        @pl.when(step > 0)
        def _():
            dlng_ref[...] += part_g
            dlnb_ref[...] += part_b

        dxhat = dvln * lng
        m1 = jnp.mean(dxhat, axis=-1, keepdims=True)
        m2 = jnp.mean(dxhat * xhat, axis=-1, keepdims=True)
        dv_ref[...] = (rstd * (dxhat - m1 - xhat * m2) * _gelu_grad(vp)).astype(BF16)

        @pl.when(step == nsteps - 1)
        def _():
            for g in range(groups):
                dw_ref[g] = jnp.where(mask, dw_acc[g], 0.0)
                db_ref[g] = jnp.broadcast_to(jnp.sum(db_acc[g], axis=-1, keepdims=True), (SGU_LEN, SGU_LEN))

    row = lambda i: (i, 0)
    outb = (SDS((s, sgw), BF16), (rb, sgw), row)
    outv = (SDS((1, sgw), F32), (1, sgw), lambda i: (0, 0))
    outm = (SDS((groups, SGU_LEN, SGU_LEN), F32), (groups, SGU_LEN, SGU_LEN), lambda i: (0, 0, 0))
    acc = pltpu.VMEM((groups, SGU_LEN, SGU_LEN), F32)
    return _call("sgu_bwd", (nsteps,),
                 _sgu_specs(proj, ln_g3, ln_b3, w_sp, b_col, l, sbw, sgw, rb) + [(ds, (rb, sgw), row)],
                 [outb, outb, outv, outv, outm, outm], body, scratch=[pltpu.VMEM((rb, sgw), F32), acc, acc])


def _merge_fwd(o, sg, w_oa, w_ob, proj, l, d, gate_off):
    s, kw = o.shape
    shard = w_oa.shape[3]
    tm, tn = _tile(s, MM_TM, 8), _tile(shard, MM_TN)
    nps = shard // tn

    def body(o_ref, s_ref, wa_ref, wb_ref, ga_ref, gb_ref, ya_ref, yb_ref, mg_ref):
        ya = _dot(o_ref[...], wa_ref[...])
        yb = _dot(s_ref[...], wb_ref[...])
        ya_ref[...] = ya
        yb_ref[...] = yb
        mg_ref[...] = (jax.nn.sigmoid(ga_ref[...]) * ya + jax.nn.sigmoid(gb_ref[...]) * yb).astype(BF16)

    act = lambda a: (a, (tm, kw), lambda i, j: (i, 0))
    wsp = lambda w: (w, (None, None, kw, tn), lambda i, j: (j // nps, l, 0, j % nps))
    gate = lambda off: (proj, (tm, tn), lambda i, j: (i, off // tn + j))
    tile = lambda dt: (SDS((s, d), dt), (tm, tn), lambda i, j: (i, j))
    return _call("merge_fwd", (s // tm, d // tn), [act(o), act(sg), wsp(w_oa), wsp(w_ob), gate(gate_off), gate(gate_off + d)],
                 [tile(F32), tile(F32), tile(BF16)], body, semantics=("parallel", "parallel"))


def _merge_bwd_epilogue(dm, ya, yb, ga, gb):
    sa, sb = jax.nn.sigmoid(ga), jax.nn.sigmoid(gb)
    return dm * ya * sa * (1.0 - sa), dm * yb * sb * (1.0 - sb), dm * sa, dm * sb


def _position():
    x, y, c = lax.axis_index("x"), lax.axis_index("y"), lax.axis_index("c")
    return x, y, c


def _all_gather(name, shard):
    def body(x_ref, out_ref, send_sems, recv_sems, local_sem):
        x, y, c = _position()
        me, sibling = (x, y, c), (x, y, 1 - c)
        chips = [(1 - x, y), (x, 1 - y), (1 - x, 1 - y)]

        def slot(px, py, pc):
            return out_ref.at[4 * px + 2 * py + pc]

        def copy(k, block, to, src=None):
            return pltpu.make_async_remote_copy(
                src_ref=slot(*block) if src is None else src, dst_ref=slot(*block),
                send_sem=send_sems.at[k], recv_sem=recv_sems.at[k], device_id=to, device_id_type=MESH)

        mine = pltpu.make_async_copy(x_ref, slot(*me), local_sem)
        mine.start()
        first = [copy(0, me, sibling, src=x_ref)]
        first += [copy(1 + j, me, (*chip, c), src=x_ref) for j, chip in enumerate(chips)]
        for cp in first:
            cp.start()
        passed = [copy(4 + j, (*chip, c), sibling) for j, chip in enumerate(chips)]
        for j, chip in enumerate(chips):
            copy(1 + j, (*chip, c), me).wait_recv()
            passed[j].start()
        copy(0, sibling, me).wait_recv()
        for j, chip in enumerate(chips):
            copy(4 + j, (*chip, 1 - c), me).wait_recv()
        for cp in first + passed:
            cp.wait_send()
        mine.wait()

    return pl.pallas_call(
        body, name=name, out_shape=SDS((N_DEV,) + shard.shape, shard.dtype),
        in_specs=[pl.BlockSpec(memory_space=pl.ANY)], out_specs=pl.BlockSpec(memory_space=pl.ANY),
        scratch_shapes=[pltpu.SemaphoreType.DMA((7,)), pltpu.SemaphoreType.DMA((7,)), pltpu.SemaphoreType.DMA(())],
    )(shard)


def _exchange_blocks(name, blocks):
    nl = len(blocks)
    shape = blocks[0].shape[1:]

    def body(*refs):
        in_refs, out_ref = refs[:nl], refs[nl]
        send_sems, recv_sems, local_sems = refs[nl + 1:]
        x, y, c = _position()
        me = 4 * x + 2 * y + c
        local = [pltpu.make_async_copy(in_refs[l].at[me], out_ref.at[me, l], local_sems.at[l]) for l in range(nl)]
        for cp in local:
            cp.start()
        sends = []
        for k in range(1, N_DEV):
            px = 1 - x if k & 4 else x
            py = 1 - y if k & 2 else y
            pc = 1 - c if k & 1 else c
            peer = 4 * px + 2 * py + pc
            for l in range(nl):
                n = (k - 1) * nl + l
                sends.append(pltpu.make_async_remote_copy(
                    src_ref=in_refs[l].at[peer], dst_ref=out_ref.at[me, l],
                    send_sem=send_sems.at[n], recv_sem=recv_sems.at[n], device_id=(px, py, pc), device_id_type=MESH))
                sends[-1].start()
        n = 0
        for k in range(1, N_DEV):
            px = 1 - x if k & 4 else x
            py = 1 - y if k & 2 else y
            pc = 1 - c if k & 1 else c
            peer = 4 * px + 2 * py + pc
            for l in range(nl):
                pltpu.make_async_remote_copy(
                    src_ref=in_refs[l].at[peer], dst_ref=out_ref.at[peer, l],
                    send_sem=send_sems.at[n], recv_sem=recv_sems.at[n], device_id=(px, py, pc), device_id_type=MESH).wait_recv()
                n += 1
        for cp in sends:
            cp.wait_send()
        for cp in local:
            cp.wait()

    nsem = (N_DEV - 1) * nl
    return pl.pallas_call(
        body, name=name, out_shape=SDS((N_DEV, nl) + shape, blocks[0].dtype),
        in_specs=[pl.BlockSpec(memory_space=pl.ANY)] * nl, out_specs=pl.BlockSpec(memory_space=pl.ANY),
        scratch_shapes=[pltpu.SemaphoreType.DMA((nsem,)), pltpu.SemaphoreType.DMA((nsem,)), pltpu.SemaphoreType.DMA((nl,))],
    )(*blocks)


def _adam_math(g, w, m, v):
    m = ADAM_B1 * m + (1.0 - ADAM_B1) * g
    v = ADAM_B2 * v + (1.0 - ADAM_B2) * (g * g)
    m_hat = m / (1.0 - ADAM_B1 ** ADAM_STEP)
    v_hat = v / (1.0 - ADAM_B2 ** ADAM_STEP)
    delta = -ADAM_LR * (m_hat / (jnp.sqrt(v_hat) + ADAM_EPS) + ADAM_WD * w)
    return delta, m, v


def _adam(name, parts, w, m, v):
    nl, r, c = w.shape
    tr, tc = _tile(r, 256, 8), _tile(c, MM_TN)

    def body(p_ref, w_ref, m_ref, v_ref, g_ref, d_ref, mo_ref, vo_ref):
        g = p_ref[0].astype(F32)
        for q in range(1, N_DEV):
            g = g + p_ref[q].astype(F32)
        delta, mn, vn = _adam_math(g, w_ref[...], m_ref[...], v_ref[...])
        g_ref[...] = g
        d_ref[...] = delta
        mo_ref[...] = mn
        vo_ref[...] = vn

    cur = lambda a: (a, (None, tr, tc), lambda l, i, j: (l, i, j))
    out = (SDS(w.shape, F32), (None, tr, tc), lambda l, i, j: (l, i, j))
    return _call(name, (nl, r // tr, c // tc),
                 [(parts, (N_DEV, None, tr, tc), lambda l, i, j: (0, l, i, j)), cur(w), cur(m), cur(v)],
                 [out, out, out, out], body, semantics=("parallel", "parallel", "parallel"))


def _adam_small(parts, w, m, v):
    r = w.shape[0]
    tr = _tile(r, 1096, 8)

    def body(p_ref, w_ref, m_ref, v_ref, g_ref, d_ref, mo_ref, vo_ref):
        g = p_ref[0]
        for q in range(1, N_DEV):
            g = g + p_ref[q]
        delta, mn, vn = _adam_math(g, w_ref[...], m_ref[...], v_ref[...])
        g_ref[...] = g
        d_ref[...] = delta
        mo_ref[...] = mn
        vo_ref[...] = vn

    cur = lambda a: (a, (tr, 128), lambda i: (i, 0))
    out = (SDS(w.shape, F32), (tr, 128), lambda i: (i, 0))
    return _call("adam_small", (r // tr,), [(parts, (N_DEV, tr, 128), lambda i: (0, i, 0)), cur(w), cur(m), cur(v)],
                 [out, out, out, out], body, semantics=("parallel",))


SMALL = ("g_mix", "g_q", "g_k", "sgu_ln_g", "sgu_ln_b", "w_spatial", "b_spatial", "g_ff")
BIG = ("w_in", "w_oa", "w_ob", "w_out", "w_ff1", "w_ff2")
ORDER = ("g_mix", "w_in", "g_q", "g_k", "sgu_ln_g", "sgu_ln_b", "w_spatial", "b_spatial", "w_oa", "w_ob", "w_out",
         "g_ff", "w_ff1", "w_ff2")


def _pack(arrs):
    return jnp.concatenate([a.reshape(-1) for a in arrs]).reshape(-1, 128)


def _unpack(packed, like):
    flat, out, off = packed.reshape(-1), [], 0
    for a in like:
        out.append(flat[off:off + a.size].reshape(a.shape))
        off += a.size
    return out


def kernel(x, g_mix, w_in, g_q, g_k, sgu_ln_g, sgu_ln_b, w_spatial, b_spatial, w_oa, w_ob, w_out, g_ff, w_ff1, w_ff2, loss_target, m_g_mix, m_w_in, m_g_q, m_g_k, m_sgu_ln_g, m_sgu_ln_b, m_w_spatial, m_b_spatial, m_w_oa, m_w_ob, m_w_out, m_g_ff, m_w_ff1, m_w_ff2, v_g_mix, v_w_in, v_g_q, v_g_k, v_sgu_ln_g, v_sgu_ln_b, v_w_spatial, v_b_spatial, v_w_oa, v_w_ob, v_w_out, v_g_ff, v_w_ff1, v_w_ff2):
    w = dict(g_mix=g_mix, w_in=w_in, g_q=g_q, g_k=g_k, sgu_ln_g=sgu_ln_g, sgu_ln_b=sgu_ln_b, w_spatial=w_spatial,
             b_spatial=b_spatial, w_oa=w_oa, w_ob=w_ob, w_out=w_out, g_ff=g_ff, w_ff1=w_ff1, w_ff2=w_ff2)
    mom = dict(g_mix=m_g_mix, w_in=m_w_in, g_q=m_g_q, g_k=m_g_k, sgu_ln_g=m_sgu_ln_g, sgu_ln_b=m_sgu_ln_b,
               w_spatial=m_w_spatial, b_spatial=m_b_spatial, w_oa=m_w_oa, w_ob=m_w_ob, w_out=m_w_out, g_ff=m_g_ff,
               w_ff1=m_w_ff1, w_ff2=m_w_ff2)
    var = dict(g_mix=v_g_mix, w_in=v_w_in, g_q=v_g_q, g_k=v_g_k, sgu_ln_g=v_sgu_ln_g, sgu_ln_b=v_sgu_ln_b,
               w_spatial=v_w_spatial, b_spatial=v_b_spatial, w_oa=v_w_oa, w_ob=v_w_ob, w_out=v_w_out, g_ff=v_g_ff,
               w_ff1=v_w_ff1, w_ff2=v_w_ff2)

    xs = x[0]
    target = loss_target[0]
    s, d = xs.shape
    depth = g_mix.shape[0]
    sbw = g_q.shape[1] * g_q.shape[2]
    sgw = sgu_ln_g.shape[1]
    groups = w_spatial.shape[1]
    d_ff = w_ff1.shape[2] * N_DEV
    n_in = w_in.shape[2] * N_DEV
    gate_off = 3 * sbw + 2 * sgw

    full = {n: _all_gather("gather_" + n, w[n].astype(BF16)) for n in BIG}

    g_mix3, g_ff3 = g_mix.reshape(depth, 1, d), g_ff.reshape(depth, 1, d)
    gq3, gk3 = g_q.reshape(depth, 1, sbw), g_k.reshape(depth, 1, sbw)
    ln_g3, ln_b3 = sgu_ln_g.reshape(depth, 1, sgw), sgu_ln_b.reshape(depth, 1, sgw)
    b_col = jnp.broadcast_to(b_spatial[..., None], b_spatial.shape + (SGU_LEN,))

    def tile_out(n, dt):
        return lambda tm, tn: [(SDS((s, n), dt), (tm, tn), lambda i, j, k: (i, j))]

    def tile_in(a):
        return lambda tm, tn: [(a, (tm, tn), lambda i, j, k: (i, j))]

    def gate_tile(proj, off, tm, tn):
        return (proj, (tm, tn), lambda i, j, k: (i, off // tn + j))

    saved = []
    cur = xs
    for l in range(depth):
        h = _rms_fwd("rms_mix", cur, g_mix3, l)
        proj = _mm_fwd("proj_in", h, _w_cols, full["w_in"], l, n_in, d, tile_out(n_in, F32), tn_div=n_in // N_DEV)[0]
        qn, kn, vb = _qkv_prep(proj, gq3, gk3, l, sbw)
        o = _attn_fwd(qn, kn, vb)
        sg = _sgu_fwd(proj, ln_g3, ln_b3, w_spatial, b_col, l, sbw, sgw)
        ya, yb, merged = _merge_fwd(o, sg, full["w_oa"], full["w_ob"], proj, l, d, gate_off)
        x1 = _mm_fwd("proj_out", merged, _w_rows, full["w_out"], l, d, d, tile_out(d, F32),
                     lambda acc, res: (res + acc,), tile_in(cur), k_div=d // N_DEV)[0]
        h2 = _rms_fwd("rms_ff", x1, g_ff3, l)
        a1, r = _mm_fwd("ff_up", h2, _w_cols, full["w_ff1"], l, d_ff, d,
                        lambda tm, tn: tile_out(d_ff, F32)(tm, tn) + tile_out(d_ff, BF16)(tm, tn),
                        lambda acc: (acc, jnp.square(jnp.maximum(acc, 0.0))), tn_div=d_ff // N_DEV)
        x2 = _mm_fwd("ff_down", r, _w_rows, full["w_ff2"], l, d, d_ff, tile_out(d, F32),
                     lambda acc, res: (res + acc,), tile_in(x1), k_div=d_ff // N_DEV)[0]
        saved.append(dict(x=cur, h=h, proj=proj, qn=qn, kn=kn, vb=vb, o=o, sg=sg, ya=ya, yb=yb, merged=merged,
                          x1=x1, h2=h2, a1=a1, r=r))
        cur = x2

    dx, dxb, sq = _loss_head(cur, target)
    loss = lax.psum(0.5 * jnp.sum(sq) / d, ("x", "y", "c"))

    gw = {n: [None] * depth for n in BIG}
    gs = {n: [None] * depth for n in SMALL}
    for l in reversed(range(depth)):
        sv = saved[l]
        da1 = _mm_bwd_x("ff_down_dx", dxb, _w_rows_t, full["w_ff2"], l, d_ff, d, d, tile_out(d_ff, BF16),
                        lambda acc, a1: (acc * (2.0 * jnp.maximum(a1, 0.0)),), tile_in(sv["a1"]), tn_div=d_ff // N_DEV)[0]
        gw["w_ff2"][l] = _mm_bwd_w("ff_down_dw", sv["r"], dxb, False)
        dh2 = _mm_bwd_x("ff_up_dx", da1, _w_cols_t, full["w_ff1"], l, d, d_ff, d_ff // N_DEV, tile_out(d, F32))[0]
        gw["w_ff1"][l] = _mm_bwd_w("ff_up_dw", sv["h2"], da1, True)
        dx1, dx1b, dg = _rms_bwd("rms_ff_bwd", sv["x1"], g_ff3, l, dh2, dx)
        gs["g_ff"][l] = dg.reshape(d)

        dga, dgb, dya, dyb = _mm_bwd_x(
            "proj_out_dx", dx1b, _w_rows_t, full["w_out"], l, d, d, d,
            lambda tm, tn: tile_out(d, BF16)(tm, tn) * 4, _merge_bwd_epilogue,
            lambda tm, tn: [(a, (tm, tn), lambda i, j, k: (i, j)) for a in (sv["ya"], sv["yb"])]
            + [gate_tile(sv["proj"], off, tm, tn) for off in (gate_off, gate_off + d)],
            tn_div=d // N_DEV)
        gw["w_out"][l] = _mm_bwd_w("proj_out_dw", sv["merged"], dx1b, False)
        do = _mm_bwd_x("oa_dx", dya, _w_cols_t, full["w_oa"], l, sbw, d, d // N_DEV, tile_out(sbw, BF16))[0]
        dsg = _mm_bwd_x("ob_dx", dyb, _w_cols_t, full["w_ob"], l, sgw, d, d // N_DEV, tile_out(sgw, F32))[0]
        gw["w_oa"][l] = _mm_bwd_w("oa_dw", sv["o"], dya, True)
        gw["w_ob"][l] = _mm_bwd_w("ob_dw", sv["sg"], dyb, True)

        dqn, dkn, dv = _attn_bwd(sv["qn"], sv["kn"], sv["vb"], do)
        dq, dk, dvb, dgq, dgk = _qk_bwd(sv["proj"], gq3, gk3, l, sbw, dqn, dkn, dv)
        gs["g_q"][l], gs["g_k"][l] = dgq.reshape(g_q.shape[1:]), dgk.reshape(g_k.shape[1:])
        du, dvs, dlng, dlnb, dwsp, dbsp = _sgu_bwd(sv["proj"], ln_g3, ln_b3, w_spatial, b_col, l, sbw, sgw, dsg)
        gs["sgu_ln_g"][l], gs["sgu_ln_b"][l] = dlng.reshape(sgw), dlnb.reshape(sgw)
        gs["w_spatial"][l], gs["b_spatial"][l] = dwsp, dbsp[:, :, 0]

        dproj = jnp.concatenate([dq, dk, dvb, du, dvs, dga, dgb], axis=1)
        dh = _mm_bwd_x("proj_in_dx", dproj, _w_cols_t, full["w_in"], l, d, n_in, n_in // N_DEV, tile_out(d, F32))[0]
        gw["w_in"][l] = _mm_bwd_w("proj_in_dw", sv["h"], dproj, True)
        dx, dxb, dg = _rms_bwd("rms_mix_bwd", sv["x"], g_mix3, l, dh, dx1)
        gs["g_mix"][l] = dg.reshape(d)

    res = {}
    for n in BIG:
        parts = _exchange_blocks("exchange_" + n, gw[n])
        res[n] = _adam("adam_" + n, parts, w[n], mom[n], var[n])
    small_parts = _all_gather("gather_small", _pack([jnp.stack(gs[n]) for n in SMALL]))
    packed = _adam_small(small_parts, _pack([w[n] for n in SMALL]), _pack([mom[n] for n in SMALL]),
                         _pack([var[n] for n in SMALL]))
    unpacked = [_unpack(p, [w[n] for n in SMALL]) for p in packed]
    for i, n in enumerate(SMALL):
        res[n] = tuple(u[i] for u in unpacked)

    outs = [loss, dx[None]]
    for which in range(4):
        outs += [res[n][which] for n in ORDER]
    return tuple(outs)
```

```python
import functools

import jax
import jax.numpy as jnp
from jax import lax
from jax.experimental import pallas as pl
from jax.experimental.pallas import tpu as pltpu

F32 = jnp.float32
BF16 = jnp.bfloat16
SDS = jax.ShapeDtypeStruct
MESH = pl.DeviceIdType.MESH

N_DEV = 8
HEAD_DIM = 128
GROUP_DIM = 128
SGU_LEN = 128
SGU_CAUSAL = 64
NORM_EPS = 1e-6
ATT_BLOCK = 128
ATT_DEAD = -110.0
ATT_CHAINS = 4

ADAM_LR, ADAM_B1, ADAM_B2, ADAM_EPS, ADAM_WD, ADAM_STEP = 0.001, 0.9, 0.999, 1e-08, 0.01, 10

ROW_TILE = 512
MM_TM, MM_TN, MM_TK = 1024, 1152, 2048


def _tile(n, target, align=128):
    best = None
    for t in range(align, min(n, target) + 1, align):
        if n % t == 0:
            best = t
    return n if best is None else best


class _Exchange:
    def __init__(self, src, gather):
        self.src, self.gather = src, gather
        self.out = SDS(((N_DEV,) + src.shape) if gather else src.shape, src.dtype)
        self.sems = [pltpu.SemaphoreType.DMA((N_DEV - 1,)), pltpu.SemaphoreType.DMA((N_DEV - 1,)),
                     pltpu.SemaphoreType.DMA(())]

    def plan(self, src_ref, out_ref, send_sems, recv_sems, local_sem):
        x, y, c = _position()
        me = 4 * x + 2 * y + c
        pick = (lambda p: src_ref) if self.gather else (lambda p: src_ref.at[p])
        local = pltpu.make_async_copy(pick(me), out_ref.at[me], local_sem)
        sends, arrivals = [], []
        for k in range(1, N_DEV):
            px = 1 - x if k & 4 else x
            py = 1 - y if k & 2 else y
            pc = 1 - c if k & 1 else c
            peer = 4 * px + 2 * py + pc
            for dst, keep in ((out_ref.at[me], sends), (out_ref.at[peer], arrivals)):
                keep.append(pltpu.make_async_remote_copy(
                    src_ref=pick(peer), dst_ref=dst, send_sem=send_sems.at[k - 1], recv_sem=recv_sems.at[k - 1],
                    device_id=(px, py, pc), device_id_type=MESH))
        return local, sends, arrivals


def _call(name, grid, ins, outs, body, scratch=(), semantics=None, jobs=(), aliases=None):
    n_in, n_out, n_scr, n_job = len(ins), len(outs), len(scratch), len(jobs)
    any_spec = pl.BlockSpec(memory_space=pl.ANY)

    def wrapped(*refs):
        own_in, job_in = refs[:n_in], refs[n_in:n_in + n_job]
        p = n_in + n_job
        own_out, job_out = refs[p:p + n_out], refs[p + n_out:p + n_out + n_job]
        p += n_out + n_job
        own_scr, job_sems = refs[p:p + n_scr], refs[p + n_scr:]
        if not jobs:
            return body(*own_in, *own_out, *own_scr)
        ids = [pl.program_id(a) for a in range(len(grid))]
        first = functools.reduce(jnp.logical_and, [i == 0 for i in ids])
        last = functools.reduce(jnp.logical_and, [i == g - 1 for i, g in zip(ids, grid)])
        plans = [jb.plan(job_in[n], job_out[n], *job_sems[3 * n:3 * n + 3]) for n, jb in enumerate(jobs)]

        @pl.when(first)
        def _():
            for local, sends, _ in plans:
                local.start()
                for cp in sends:
                    cp.start()

        body(*own_in, *own_out, *own_scr)

        @pl.when(last)
        def _():
            for local, sends, arrivals in plans:
                for cp in arrivals:
                    cp.wait_recv()
                for cp in sends:
                    cp.wait_send()
                local.wait()

    spec = lambda blk, im: any_spec if blk is None else pl.BlockSpec(blk, im)
    res = pl.pallas_call(
        wrapped, name=name, grid=grid,
        in_specs=[spec(blk, im) for _, blk, im in ins] + [any_spec] * n_job,
        out_specs=[spec(blk, im) for _, blk, im in outs] + [any_spec] * n_job,
        out_shape=[s for s, _, _ in outs] + [jb.out for jb in jobs],
        scratch_shapes=list(scratch) + [sem for jb in jobs for sem in jb.sems],
        input_output_aliases=aliases or {},
        compiler_params=pltpu.CompilerParams(
            dimension_semantics=("arbitrary",) * len(grid) if jobs or not semantics else semantics),
    )(*[a for a, _, _ in ins], *[jb.src for jb in jobs])
    return res


def _dot(a, b, dims=((1,), (0,))):
    return lax.dot_general(a, b, (dims, ((), ())), preferred_element_type=F32)


NN = ((1,), (0,))
NT = ((1,), (1,))
TN = ((0,), (0,))


def _matmul(name, grid, a, b, dims, outs, epilogue=None, extras=(), jobs=()):
    nk = grid[2]
    n_ex, n_out = len(extras), len(outs)
    acc_shape = tuple(d for d in outs[0][1] if d is not None)

    def body(*refs):
        a_ref, b_ref = refs[0], refs[1]
        ex_refs = refs[2:2 + n_ex]
        out_refs = refs[2 + n_ex:2 + n_ex + n_out]
        part = _dot(a_ref[...].astype(BF16), b_ref[...].astype(BF16), dims)

        def finish(acc):
            vals = (acc,) if epilogue is None else epilogue(acc, *[r[...] for r in ex_refs])
            for r, v in zip(out_refs, vals):
                r[...] = v.astype(r.dtype)

        if nk == 1:
            finish(part)
        else:
            acc_ref = refs[-1]
            k = pl.program_id(2)

            @pl.when(k == 0)
            def _():
                acc_ref[...] = part

            @pl.when(k > 0)
            def _():
                acc_ref[...] += part

            @pl.when(k == nk - 1)
            def _():
                finish(acc_ref[...])

    return _call(name, grid, [a, b, *extras], list(outs), body,
                 scratch=[pltpu.VMEM(acc_shape, F32)] if nk > 1 else [],
                 semantics=("parallel", "parallel", "arbitrary"), jobs=jobs)


def _w_cols(w, tk, tn):
    nps = w.shape[2] // tn
    return (w, (None, tk, tn), lambda i, j, k: (j // nps, k, j % nps))


def _w_rows(w, tk, tn):
    kps = w.shape[1] // tk
    return (w, (None, tk, tn), lambda i, j, k: (k // kps, k % kps, j))


def _w_cols_t(w, tn, tk):
    kps = w.shape[2] // tk
    return (w, (None, tn, tk), lambda i, j, k: (k // kps, j, k % kps))


def _w_rows_t(w, tn, tk):
    nps = w.shape[1] // tn
    return (w, (None, tn, tk), lambda i, j, k: (j // nps, j % nps, k))


def _mm_fwd(name, a, wspec_fn, w, n_out, k_dim, outs_fn, epilogue=None, extras_fn=None, tn_div=None, k_div=None, jobs=()):
    m = a.shape[0]
    tm = _tile(m, MM_TM, 8)
    shard_n = tn_div if tn_div is not None else n_out
    tn = _tile(shard_n, MM_TN)
    tk = _tile(k_div if k_div is not None else k_dim, MM_TK)
    grid = (m // tm, n_out // tn, k_dim // tk)
    extras = extras_fn(tm, tn) if extras_fn else ()
    return _matmul(name, grid, (a, (tm, tk), lambda i, j, k: (i, k)), wspec_fn(w, tk, tn), NN,
                   outs_fn(tm, tn), epilogue, extras, jobs)


def _mm_bwd_x(name, a, wspec_fn, w, n_out, k_dim, k_div, outs_fn, epilogue=None, extras_fn=None, tn_div=None, jobs=()):
    m = a.shape[0]
    tm = _tile(m, MM_TM, 8)
    tn = _tile(tn_div if tn_div is not None else n_out, 1024)
    tk = _tile(k_div, MM_TK)
    grid = (m // tm, n_out // tn, k_dim // tk)
    extras = extras_fn(tm, tn) if extras_fn else ()
    return _matmul(name, grid, (a, (tm, tk), lambda i, j, k: (i, k)), wspec_fn(w, tn, tk), NT,
                   outs_fn(tm, tn), epilogue, extras, jobs)


def _mm_bwd_w(name, a, b, col_sharded):
    m, ka = a.shape
    n = b.shape[1]
    ts = _tile(m, 1024, 8)
    if col_sharded:
        shard = n // N_DEV
        ti, tj = _tile(ka, 1024), _tile(shard, MM_TN)
        nps = shard // tj
        out = (SDS((N_DEV, ka, shard), BF16), (None, ti, tj), lambda i, j, k: (j // nps, i, j % nps))
    else:
        shard = ka // N_DEV
        ti, tj = _tile(shard, 1024), _tile(n, 1024)
        ips = shard // ti
        out = (SDS((N_DEV, shard, n), BF16), (None, ti, tj), lambda i, j, k: (i // ips, i % ips, j))
    grid = (ka // ti, n // tj, m // ts)
    return _matmul(name, grid, (a, (ts, ti), lambda i, j, k: (k, i)), (b, (ts, tj), lambda i, j, k: (k, j)), TN, [out])[0]


def _rms_fwd(name, x, g3, l):
    s, d = x.shape
    ts = _tile(s, ROW_TILE, 8)

    def body(x_ref, g_ref, h_ref):
        xv = x_ref[...]
        r = lax.rsqrt(jnp.mean(xv * xv, axis=-1, keepdims=True) + NORM_EPS)
        h_ref[...] = (xv * r * g_ref[...]).astype(BF16)

    return _call(name, (s // ts,), [(x, (ts, d), lambda i: (i, 0)), (g3, (None, 1, d), lambda i: (l, 0, 0))],
                 [(SDS((s, d), BF16), (ts, d), lambda i: (i, 0))], body, semantics=("parallel",))[0]


def _rms_bwd(name, x, g3, l, dh, dres):
    s, d = x.shape
    ts = _tile(s, ROW_TILE, 8)

    def body(x_ref, g_ref, dh_ref, dres_ref, dx_ref, dxb_ref, dg_ref):
        xv = x_ref[...]
        r = lax.rsqrt(jnp.mean(xv * xv, axis=-1, keepdims=True) + NORM_EPS)
        xhat = xv * r
        dhv = dh_ref[...]
        gy = dhv * g_ref[...]
        m = jnp.mean(gy * xhat, axis=-1, keepdims=True)
        dx = dres_ref[...] + r * (gy - xhat * m)
        dx_ref[...] = dx
        dxb_ref[...] = dx.astype(BF16)
        part = jnp.sum(dhv * xhat, axis=0, keepdims=True)

        @pl.when(pl.program_id(0) == 0)
        def _():
            dg_ref[...] = part

        @pl.when(pl.program_id(0) > 0)
        def _():
            dg_ref[...] += part

    row = lambda i: (i, 0)
    return _call(name, (s // ts,),
                 [(x, (ts, d), row), (g3, (None, 1, d), lambda i: (l, 0, 0)), (dh, (ts, d), row), (dres, (ts, d), row)],
                 [(SDS((s, d), F32), (ts, d), row), (SDS((s, d), BF16), (ts, d), row),
                  (SDS((1, d), F32), (1, d), lambda i: (0, 0))], body)


def _loss_head(y, target):
    s, d = y.shape
    ts = _tile(s, ROW_TILE, 8)

    def body(y_ref, t_ref, dy_ref, dyb_ref, sq_ref):
        diff = y_ref[...] - t_ref[...]
        dy = diff / d
        dy_ref[...] = dy
        dyb_ref[...] = dy.astype(BF16)
        part = jnp.sum(diff * diff, axis=0, keepdims=True)

        @pl.when(pl.program_id(0) == 0)
        def _():
            sq_ref[...] = part

        @pl.when(pl.program_id(0) > 0)
        def _():
            sq_ref[...] += part

    row = lambda i: (i, 0)
    return _call("loss_head", (s // ts,), [(y, (ts, d), row), (target, (ts, d), row)],
                 [(SDS((s, d), F32), (ts, d), row), (SDS((s, d), BF16), (ts, d), row),
                  (SDS((1, d), F32), (1, d), lambda i: (0, 0))], body)


def _qkv_prep(proj, gq3, gk3, l, sbw):
    s = proj.shape[0]
    ts = _tile(s, ROW_TILE, 8)
    heads = sbw // HEAD_DIM

    def body(q_ref, k_ref, v_ref, gq_ref, gk_ref, qn_ref, kn_ref, vb_ref):
        for h in range(heads):
            sl = slice(h * HEAD_DIM, (h + 1) * HEAD_DIM)
            for src, g_ref, dst in ((q_ref, gq_ref, qn_ref), (k_ref, gk_ref, kn_ref)):
                t = src[:, sl]
                r = lax.rsqrt(jnp.mean(t * t, axis=-1, keepdims=True) + NORM_EPS)
                dst[:, sl] = (t * r * g_ref[:, sl]).astype(BF16)
        vb_ref[...] = v_ref[...].astype(BF16)

    gspec = lambda g: (g, (None, 1, sbw), lambda i: (l, 0, 0))
    col = lambda c: (proj, (ts, sbw), lambda i: (i, c))
    out = (SDS((s, sbw), BF16), (ts, sbw), lambda i: (i, 0))
    return _call("qkv_prep", (s // ts,), [col(0), col(1), col(2), gspec(gq3), gspec(gk3)], [out, out, out], body,
                 semantics=("parallel",))


def _qk_bwd(proj, gq3, gk3, l, sbw, dqn, dkn, dv):
    s = proj.shape[0]
    ts = _tile(s, ROW_TILE, 8)
    heads = sbw // HEAD_DIM

    def body(q_ref, k_ref, gq_ref, gk_ref, dqn_ref, dkn_ref, dv_ref, dq_ref, dk_ref, dvb_ref, dgq_ref, dgk_ref):
        first = pl.program_id(0) == 0
        for src, g_ref, dn_ref, dst, dg_ref in ((q_ref, gq_ref, dqn_ref, dq_ref, dgq_ref),
                                                (k_ref, gk_ref, dkn_ref, dk_ref, dgk_ref)):
            for h in range(heads):
                sl = slice(h * HEAD_DIM, (h + 1) * HEAD_DIM)
                t = src[:, sl]
                r = lax.rsqrt(jnp.mean(t * t, axis=-1, keepdims=True) + NORM_EPS)
                xhat = t * r
                dn = dn_ref[:, sl]
                gy = dn * g_ref[:, sl]
                m = jnp.mean(gy * xhat, axis=-1, keepdims=True)
                dst[:, sl] = (r * (gy - xhat * m)).astype(BF16)
                part = jnp.sum(dn * xhat, axis=0, keepdims=True)

                @pl.when(first)
                def _():
                    dg_ref[:, sl] = part

                @pl.when(jnp.logical_not(first))
                def _():
                    dg_ref[:, sl] += part
        dvb_ref[...] = dv_ref[...].astype(BF16)

    gspec = lambda g: (g, (None, 1, sbw), lambda i: (l, 0, 0))
    col = lambda c: (proj, (ts, sbw), lambda i: (i, c))
    row = lambda a: (a, (ts, sbw), lambda i: (i, 0))
    outb = (SDS((s, sbw), BF16), (ts, sbw), lambda i: (i, 0))
    outg = (SDS((1, sbw), F32), (1, sbw), lambda i: (0, 0))
    return _call("qk_bwd", (s // ts,), [col(0), col(1), gspec(gq3), gspec(gk3), row(dqn), row(dkn), row(dv)],
                 [outb, outb, outb, outg, outg], body)


def _softplus(z):
    return jnp.maximum(z, 0.0) + jnp.log1p(jnp.exp(-jnp.abs(z)))


def _tri_dot(x, tri):
    hi = x.astype(BF16)
    lo = (x - hi.astype(F32)).astype(BF16)
    return _dot(hi, tri) + _dot(lo, tri)


def _att_common(tb):
    rows = lax.broadcasted_iota(jnp.int32, (tb, tb), 0)
    cols = lax.broadcasted_iota(jnp.int32, (tb, tb), 1)
    suffix = (rows >= cols).astype(BF16)
    prefix = (rows <= cols).astype(BF16)
    return rows, cols, suffix, prefix


def _key_norm_max(k_ref, kmax_ref):
    kf = k_ref[...].astype(F32)
    n2 = jnp.sum(kf * kf, axis=1, keepdims=True)
    kmax_ref[...] = jnp.broadcast_to(jnp.max(n2, axis=0, keepdims=True), kmax_ref.shape)


def _att_reach(q, kmax_ref, scale):
    qf = q.astype(F32)
    qn2 = jnp.sum(qf * qf, axis=1, keepdims=True)
    return scale * jnp.sqrt(qn2 * kmax_ref[0:1, 0:1]) * 1.001 + 1e-3


def _att_chains(s, tb):
    return ATT_CHAINS if s % (ATT_CHAINS * tb) == 0 else 1


def _att_block(diag, j, tb, rows, cols):
    kb = diag - j
    off = pl.multiple_of(jnp.maximum(kb, 0) * tb, tb)
    limit = jnp.where(kb >= 0, diag * tb, -(1 << 30))
    return off, (cols + kb * tb) < (rows + limit)


def _attn_fwd(qn, kn, vb, jobs=()):
    s, sbw = qn.shape
    heads = sbw // HEAD_DIM
    tb = _tile(s, ATT_BLOCK, 8)
    nch = _att_chains(s, tb)
    tq = nch * tb
    scale = HEAD_DIM ** -0.5

    def body(q_ref, k_ref, v_ref, o_ref, kmax_ref):
        qi = pl.program_id(1)

        @pl.when(qi == 0)
        def _():
            _key_norm_max(k_ref, kmax_ref)

        rows, cols, suffix, _ = _att_common(tb)
        qs = [q_ref[c * tb:(c + 1) * tb, :] for c in range(nch)]
        reach = [_att_reach(q, kmax_ref, scale) for q in qs]
        diag = [qi * nch + c for c in range(nch)]

        def cond(carry):
            return jnp.logical_and(carry[0] <= diag[-1], carry[1])

        def step(carry):
            j, _, runs, accs = carry
            nruns, naccs, alive = [], [], None
            for c in range(nch):
                off, past = _att_block(diag[c], j, tb, rows, cols)
                z = _dot(qs[c], k_ref[pl.ds(off, tb), :], NT) * scale
                keep = jnp.where(past, -_softplus(z), 0.0)
                csum = _tri_dot(keep, suffix)
                a = jnp.where(past, jnp.exp(z + csum + runs[c]), 0.0)
                naccs.append(accs[c] + _dot(a.astype(BF16), v_ref[pl.ds(off, tb), :]))
                nruns.append(runs[c] + csum[:, 0:1])
                m = jnp.max(nruns[c] + reach[c])
                alive = m if alive is None else jnp.maximum(alive, m)
            return j + 1, alive > ATT_DEAD, tuple(nruns), tuple(naccs)

        init = (jnp.int32(0), jnp.bool_(True), tuple(jnp.zeros((tb, 1), F32) for _ in range(nch)),
                tuple(jnp.zeros((tb, HEAD_DIM), F32) for _ in range(nch)))
        accs = lax.while_loop(cond, step, init)[3]
        for c in range(nch):
            o_ref[c * tb:(c + 1) * tb, :] = accs[c].astype(o_ref.dtype)

    qspec = lambda a: (a, (tq, HEAD_DIM), lambda h, i: (i, h))
    full = lambda a: (a, (s, HEAD_DIM), lambda h, i: (0, h))
    return _call("attn_fwd", (heads, s // tq), [qspec(qn), full(kn), full(vb)],
                 [(SDS((s, sbw), BF16), (tq, HEAD_DIM), lambda h, i: (i, h))], body,
                 scratch=[pltpu.VMEM((8, 128), F32)], semantics=("parallel", "arbitrary"), jobs=jobs)


def _attn_bwd(qn, kn, vb, do, jobs=()):
    s, sbw = qn.shape
    heads = sbw // HEAD_DIM
    tb = _tile(s, ATT_BLOCK, 8)
    nch = _att_chains(s, tb)
    tq = nch * tb
    scale = HEAD_DIM ** -0.5

    def body(q_ref, k_ref, v_ref, do_ref, dq_ref, dk_ref, dv_ref, kmax_ref, g_ref):
        qi = pl.program_id(1)

        @pl.when(qi == 0)
        def _():
            _key_norm_max(k_ref, kmax_ref)
            dk_ref[...] = jnp.zeros_like(dk_ref)
            dv_ref[...] = jnp.zeros_like(dv_ref)

        rows, cols, suffix, prefix = _att_common(tb)
        qs = [q_ref[c * tb:(c + 1) * tb, :] for c in range(nch)]
        dos = [do_ref[c * tb:(c + 1) * tb, :] for c in range(nch)]
        reach = [_att_reach(q, kmax_ref, scale) for q in qs]
        diag = [qi * nch + c for c in range(nch)]

        def cond(carry):
            return jnp.logical_and(carry[0] <= diag[-1], carry[1])

        def sweep_back(carry):
            j, _, runs = carry
            nruns, alive = [], None
            for c in range(nch):
                off, past = _att_block(diag[c], j, tb, rows, cols)
                z = _dot(qs[c], k_ref[pl.ds(off, tb), :], NT) * scale
                keep = jnp.where(past, -_softplus(z), 0.0)
                csum = _tri_dot(keep, suffix)
                a = jnp.where(past, jnp.exp(z + csum + runs[c]), 0.0)
                g_ref[c, j] = a * _dot(dos[c], v_ref[pl.ds(off, tb), :], NT)
                dv_ref[pl.ds(off, tb), :] += _dot(a.astype(BF16), dos[c], TN)
                nruns.append(runs[c] + csum[:, 0:1])
                m = jnp.max(nruns[c] + reach[c])
                alive = m if alive is None else jnp.maximum(alive, m)
            return j + 1, alive > ATT_DEAD, tuple(nruns)

        zeros = tuple(jnp.zeros((tb, 1), F32) for _ in range(nch))
        nsteps = lax.while_loop(cond, sweep_back, (jnp.int32(0), jnp.bool_(True), zeros))[0]

        def sweep_fwd(i, carry):
            psums, dqs = carry
            j = nsteps - 1 - i
            npsums, ndqs = [], []
            for c in range(nch):
                off, past = _att_block(diag[c], j, tb, rows, cols)
                k = k_ref[pl.ds(off, tb), :]
                sg = jax.nn.sigmoid(_dot(qs[c], k, NT) * scale)
                g = g_ref[c, j]
                pin = _tri_dot(g, prefix)
                dz = jnp.where(past, g - sg * (psums[c] + pin), 0.0)
                dzb = (dz * scale).astype(BF16)
                ndqs.append(dqs[c] + _dot(dzb, k))
                dk_ref[pl.ds(off, tb), :] += _dot(dzb, qs[c], TN)
                npsums.append(psums[c] + pin[:, tb - 1:tb])
            return tuple(npsums), tuple(ndqs)

        init = (zeros, tuple(jnp.zeros((tb, HEAD_DIM), F32) for _ in range(nch)))
        dqs = lax.fori_loop(0, nsteps, sweep_fwd, init)[1]
        for c in range(nch):
            dq_ref[c * tb:(c + 1) * tb, :] = dqs[c]

    qspec = lambda a: (a, (tq, HEAD_DIM), lambda h, i: (i, h))
    full = lambda a: (a, (s, HEAD_DIM), lambda h, i: (0, h))
    outq = (SDS((s, sbw), F32), (tq, HEAD_DIM), lambda h, i: (i, h))
    outf = (SDS((s, sbw), F32), (s, HEAD_DIM), lambda h, i: (0, h))
    return _call("attn_bwd", (heads, s // tq), [qspec(qn), full(kn), full(vb), qspec(do)], [outq, outf, outf], body,
                 scratch=[pltpu.VMEM((8, 128), F32), pltpu.VMEM((nch, s // tb, tb, tb), F32)],
                 semantics=("parallel", "arbitrary"), jobs=jobs)


_SQRT_HALF = 0.7071067811865476
_INV_SQRT_2PI = 0.3989422804014327


def _gelu(x):
    return 0.5 * x * (1.0 + lax.erf(x * _SQRT_HALF))


def _gelu_grad(x):
    return 0.5 * (1.0 + lax.erf(x * _SQRT_HALF)) + x * jnp.exp(-0.5 * x * x) * _INV_SQRT_2PI


def _sgu_mask():
    i = lax.broadcasted_iota(jnp.int32, (SGU_LEN, SGU_LEN), 0) // SGU_CAUSAL
    j = lax.broadcasted_iota(jnp.int32, (SGU_LEN, SGU_LEN), 1) // SGU_CAUSAL
    return j <= i


def _sgu_norm(vg, lng, lnb):
    mu = jnp.mean(vg, axis=-1, keepdims=True)
    xc = vg - mu
    rstd = lax.rsqrt(jnp.mean(xc * xc, axis=-1, keepdims=True) + NORM_EPS)
    xhat = xc * rstd
    return xhat, rstd, xhat * lng + lnb


def _sgu_specs(proj, ln_g3, ln_b3, w_sp, b_col, l, sbw, sgw, rb):
    groups = sgw // GROUP_DIM
    ucol, vcol = 3 * sbw // sgw, 3 * sbw // sgw + 1
    vec = lambda a: (a, (None, 1, sgw), lambda i: (l, 0, 0))
    mat = lambda a: (a, (None, groups, SGU_LEN, SGU_LEN), lambda i: (l, 0, 0, 0))
    return [(proj, (rb, sgw), lambda i: (i, ucol)), (proj, (rb, sgw), lambda i: (i, vcol)),
            vec(ln_g3), vec(ln_b3), mat(w_sp), mat(b_col)]


def _sgu_fwd(proj, ln_g3, ln_b3, w_sp, b_col, l, sbw, sgw):
    s = proj.shape[0]
    rb = _tile(s, 2 * SGU_LEN, SGU_LEN)
    groups = sgw // GROUP_DIM

    def body(u_ref, v_ref, lng_ref, lnb_ref, w_ref, b_ref, s_ref):
        mask = _sgu_mask()
        _, _, vln = _sgu_norm(_gelu(v_ref[...]), lng_ref[...], lnb_ref[...])
        vb = vln.astype(BF16)
        for g in range(groups):
            cs = slice(g * GROUP_DIM, (g + 1) * GROUP_DIM)
            wm = jnp.where(mask, w_ref[g], 0.0).astype(BF16)
            for c in range(rb // SGU_LEN):
                rs = slice(c * SGU_LEN, (c + 1) * SGU_LEN)
                mixed = _dot(wm, vb[rs, cs]) + b_ref[g]
                s_ref[rs, cs] = (_gelu(u_ref[rs, cs]) * mixed).astype(BF16)

    return _call("sgu_fwd", (s // rb,), _sgu_specs(proj, ln_g3, ln_b3, w_sp, b_col, l, sbw, sgw, rb),
                 [(SDS((s, sgw), BF16), (rb, sgw), lambda i: (i, 0))], body, semantics=("parallel",))[0]


def _sgu_bwd(proj, ln_g3, ln_b3, w_sp, b_col, l, sbw, sgw, ds):
    s = proj.shape[0]
    rb = _tile(s, 2 * SGU_LEN, SGU_LEN)
    groups = sgw // GROUP_DIM
    nsteps = s // rb

    def body(u_ref, v_ref, lng_ref, lnb_ref, w_ref, b_ref, ds_ref,
             du_ref, dv_ref, dlng_ref, dlnb_ref, dw_ref, db_ref, dvln_ref, dw_acc, db_acc):
        step = pl.program_id(0)

        @pl.when(step == 0)
        def _():
            dw_acc[...] = jnp.zeros_like(dw_acc)
            db_acc[...] = jnp.zeros_like(db_acc)

        mask = _sgu_mask()
        vp = v_ref[...]
        lng = lng_ref[...]
        xhat, rstd, vln = _sgu_norm(_gelu(vp), lng, lnb_ref[...])
        vb = vln.astype(BF16)
        for g in range(groups):
            cs = slice(g * GROUP_DIM, (g + 1) * GROUP_DIM)
            wm = jnp.where(mask, w_ref[g], 0.0).astype(BF16)
            for c in range(rb // SGU_LEN):
                rs = slice(c * SGU_LEN, (c + 1) * SGU_LEN)
                vbp = vb[rs, cs]
                mixed = _dot(wm, vbp) + b_ref[g]
                up = u_ref[rs, cs]
                dsp = ds_ref[rs, cs]
                du_ref[rs, cs] = (dsp * mixed * _gelu_grad(up)).astype(BF16)
                dmixed = dsp * _gelu(up)
                dmb = dmixed.astype(BF16)
                dvln_ref[rs, cs] = _dot(wm, dmb, TN)
                dw_acc[g] += _dot(dmb, vbp, NT)
                db_acc[g] += dmixed
        dvln = dvln_ref[...]
        part_g = jnp.sum(dvln * xhat, axis=0, keepdims=True)
        part_b = jnp.sum(dvln, axis=0, keepdims=True)

        @pl.when(step == 0)
        def _():
            dlng_ref[...] = part_g
            dlnb_ref[...] = part_b

        @pl.when(step > 0)
        def _():
            dlng_ref[...] += part_g
            dlnb_ref[...] += part_b

        dxhat = dvln * lng
        m1 = jnp.mean(dxhat, axis=-1, keepdims=True)
        m2 = jnp.mean(dxhat * xhat, axis=-1, keepdims=True)
        dv_ref[...] = (rstd * (dxhat - m1 - xhat * m2) * _gelu_grad(vp)).astype(BF16)

        @pl.when(step == nsteps - 1)
        def _():
            for g in range(groups):
                dw_ref[g] = jnp.where(mask, dw_acc[g], 0.0)
                db_ref[g] = jnp.broadcast_to(jnp.sum(db_acc[g], axis=-1, keepdims=True), (SGU_LEN, SGU_LEN))

    row = lambda i: (i, 0)
    outb = (SDS((s, sgw), BF16), (rb, sgw), row)
    outv = (SDS((1, sgw), F32), (1, sgw), lambda i: (0, 0))
    outm = (SDS((groups, SGU_LEN, SGU_LEN), F32), (groups, SGU_LEN, SGU_LEN), lambda i: (0, 0, 0))
    acc = pltpu.VMEM((groups, SGU_LEN, SGU_LEN), F32)
    return _call("sgu_bwd", (nsteps,),
                 _sgu_specs(proj, ln_g3, ln_b3, w_sp, b_col, l, sbw, sgw, rb) + [(ds, (rb, sgw), row)],
                 [outb, outb, outv, outv, outm, outm], body, scratch=[pltpu.VMEM((rb, sgw), F32), acc, acc])


def _merge_fwd(o, sg, w_oa, w_ob, proj, d, gate_off):
    s, kw = o.shape
    shard = w_oa.shape[2]
    tm, tn = _tile(s, MM_TM, 8), _tile(shard, MM_TN)
    nps = shard // tn

    def body(o_ref, s_ref, wa_ref, wb_ref, ga_ref, gb_ref, ya_ref, yb_ref, mg_ref):
        ya = _dot(o_ref[...], wa_ref[...])
        yb = _dot(s_ref[...], wb_ref[...])
        ya_ref[...] = ya
        yb_ref[...] = yb
        mg_ref[...] = (jax.nn.sigmoid(ga_ref[...]) * ya + jax.nn.sigmoid(gb_ref[...]) * yb).astype(BF16)

    act = lambda a: (a, (tm, kw), lambda i, j: (i, 0))
    wsp = lambda w: (w, (None, kw, tn), lambda i, j: (j // nps, 0, j % nps))
    gate = lambda off: (proj, (tm, tn), lambda i, j: (i, off // tn + j))
    tile = lambda dt: (SDS((s, d), dt), (tm, tn), lambda i, j: (i, j))
    return _call("merge_fwd", (s // tm, d // tn), [act(o), act(sg), wsp(w_oa), wsp(w_ob), gate(gate_off), gate(gate_off + d)],
                 [tile(F32), tile(F32), tile(BF16)], body, semantics=("parallel", "parallel"))


def _merge_bwd_epilogue(dm, ya, yb, ga, gb):
    sa, sb = jax.nn.sigmoid(ga), jax.nn.sigmoid(gb)
    return dm * ya * sa * (1.0 - sa), dm * yb * sb * (1.0 - sb), dm * sa, dm * sb


def _position():
    x, y, c = lax.axis_index("x"), lax.axis_index("y"), lax.axis_index("c")
    return x, y, c


def _all_gather(name, shard):
    def body(x_ref, out_ref, send_sems, recv_sems, local_sem):
        x, y, c = _position()
        me, sibling = (x, y, c), (x, y, 1 - c)
        chips = [(1 - x, y), (x, 1 - y), (1 - x, 1 - y)]

        def slot(px, py, pc):
            return out_ref.at[4 * px + 2 * py + pc]

        def copy(k, block, to, src=None):
            return pltpu.make_async_remote_copy(
                src_ref=slot(*block) if src is None else src, dst_ref=slot(*block),
                send_sem=send_sems.at[k], recv_sem=recv_sems.at[k], device_id=to, device_id_type=MESH)

        mine = pltpu.make_async_copy(x_ref, slot(*me), local_sem)
        mine.start()
        first = [copy(0, me, sibling, src=x_ref)]
        first += [copy(1 + j, me, (*chip, c), src=x_ref) for j, chip in enumerate(chips)]
        for cp in first:
            cp.start()
        passed = [copy(4 + j, (*chip, c), sibling) for j, chip in enumerate(chips)]
        for j, chip in enumerate(chips):
            copy(1 + j, (*chip, c), me).wait_recv()
            passed[j].start()
        copy(0, sibling, me).wait_recv()
        for j, chip in enumerate(chips):
            copy(4 + j, (*chip, 1 - c), me).wait_recv()
        for cp in first + passed:
            cp.wait_send()
        mine.wait()

    return pl.pallas_call(
        body, name=name, out_shape=SDS((N_DEV,) + shard.shape, shard.dtype),
        in_specs=[pl.BlockSpec(memory_space=pl.ANY)], out_specs=pl.BlockSpec(memory_space=pl.ANY),
        scratch_shapes=[pltpu.SemaphoreType.DMA((7,)), pltpu.SemaphoreType.DMA((7,)), pltpu.SemaphoreType.DMA(())],
    )(shard)


def _adam_math(g, w, m, v):
    m = ADAM_B1 * m + (1.0 - ADAM_B1) * g
    v = ADAM_B2 * v + (1.0 - ADAM_B2) * (g * g)
    m_hat = m / (1.0 - ADAM_B1 ** ADAM_STEP)
    v_hat = v / (1.0 - ADAM_B2 ** ADAM_STEP)
    delta = -ADAM_LR * (m_hat / (jnp.sqrt(v_hat) + ADAM_EPS) + ADAM_WD * w)
    return delta, m, v


def _adam(name, parts, w, m, v, l, prev):
    _, r, c = w.shape
    tr, tc = _tile(r, 256, 8), _tile(c, MM_TN)

    def body(p_ref, w_ref, m_ref, v_ref, *rest):
        g_ref, d_ref, mo_ref, vo_ref = rest[-4:]
        g = p_ref[0].astype(F32)
        for q in range(1, N_DEV):
            g = g + p_ref[q].astype(F32)
        delta, mn, vn = _adam_math(g, w_ref[...], m_ref[...], v_ref[...])
        g_ref[...] = g
        d_ref[...] = delta
        mo_ref[...] = mn
        vo_ref[...] = vn

    cur = lambda a: (a, (None, tr, tc), lambda i, j: (l, i, j))
    out = (SDS(w.shape, F32), (None, tr, tc), lambda i, j: (l, i, j))
    ins = [(parts, (N_DEV, tr, tc), lambda i, j: (0, i, j)), cur(w), cur(m), cur(v)]
    aliases = None
    if prev is not None:
        ins += [(p, None, None) for p in prev]
        aliases = {4 + n: n for n in range(4)}
    return _call(name, (r // tr, c // tc), ins, [out, out, out, out], body, semantics=("parallel", "parallel"),
                 aliases=aliases)


def _adam_small(parts, w, m, v):
    r = w.shape[0]
    tr = _tile(r, 1096, 8)

    def body(p_ref, w_ref, m_ref, v_ref, g_ref, d_ref, mo_ref, vo_ref):
        g = p_ref[0]
        for q in range(1, N_DEV):
            g = g + p_ref[q]
        delta, mn, vn = _adam_math(g, w_ref[...], m_ref[...], v_ref[...])
        g_ref[...] = g
        d_ref[...] = delta
        mo_ref[...] = mn
        vo_ref[...] = vn

    cur = lambda a: (a, (tr, 128), lambda i: (i, 0))
    out = (SDS(w.shape, F32), (tr, 128), lambda i: (i, 0))
    return _call("adam_small", (r // tr,), [(parts, (N_DEV, tr, 128), lambda i: (0, i, 0)), cur(w), cur(m), cur(v)],
                 [out, out, out, out], body, semantics=("parallel",))


SMALL = ("g_mix", "g_q", "g_k", "sgu_ln_g", "sgu_ln_b", "w_spatial", "b_spatial", "g_ff")
BIG = ("w_in", "w_oa", "w_ob", "w_out", "w_ff1", "w_ff2")
ORDER = ("g_mix", "w_in", "g_q", "g_k", "sgu_ln_g", "sgu_ln_b", "w_spatial", "b_spatial", "w_oa", "w_ob", "w_out",
         "g_ff", "w_ff1", "w_ff2")


def _pack(arrs):
    return jnp.concatenate([a.reshape(-1) for a in arrs]).reshape(-1, 128)


def _unpack(packed, like):
    flat, out, off = packed.reshape(-1), [], 0
    for a in like:
        out.append(flat[off:off + a.size].reshape(a.shape))
        off += a.size
    return out


def kernel(x, g_mix, w_in, g_q, g_k, sgu_ln_g, sgu_ln_b, w_spatial, b_spatial, w_oa, w_ob, w_out, g_ff, w_ff1, w_ff2, loss_target, m_g_mix, m_w_in, m_g_q, m_g_k, m_sgu_ln_g, m_sgu_ln_b, m_w_spatial, m_b_spatial, m_w_oa, m_w_ob, m_w_out, m_g_ff, m_w_ff1, m_w_ff2, v_g_mix, v_w_in, v_g_q, v_g_k, v_sgu_ln_g, v_sgu_ln_b, v_w_spatial, v_b_spatial, v_w_oa, v_w_ob, v_w_out, v_g_ff, v_w_ff1, v_w_ff2):
    w = dict(g_mix=g_mix, w_in=w_in, g_q=g_q, g_k=g_k, sgu_ln_g=sgu_ln_g, sgu_ln_b=sgu_ln_b, w_spatial=w_spatial,
             b_spatial=b_spatial, w_oa=w_oa, w_ob=w_ob, w_out=w_out, g_ff=g_ff, w_ff1=w_ff1, w_ff2=w_ff2)
    mom = dict(g_mix=m_g_mix, w_in=m_w_in, g_q=m_g_q, g_k=m_g_k, sgu_ln_g=m_sgu_ln_g, sgu_ln_b=m_sgu_ln_b,
               w_spatial=m_w_spatial, b_spatial=m_b_spatial, w_oa=m_w_oa, w_ob=m_w_ob, w_out=m_w_out, g_ff=m_g_ff,
               w_ff1=m_w_ff1, w_ff2=m_w_ff2)
    var = dict(g_mix=v_g_mix, w_in=v_w_in, g_q=v_g_q, g_k=v_g_k, sgu_ln_g=v_sgu_ln_g, sgu_ln_b=v_sgu_ln_b,
               w_spatial=v_w_spatial, b_spatial=v_b_spatial, w_oa=v_w_oa, w_ob=v_w_ob, w_out=v_w_out, g_ff=v_g_ff,
               w_ff1=v_w_ff1, w_ff2=v_w_ff2)

    xs = x[0]
    target = loss_target[0]
    s, d = xs.shape
    depth = g_mix.shape[0]
    sbw = g_q.shape[1] * g_q.shape[2]
    sgw = sgu_ln_g.shape[1]
    groups = w_spatial.shape[1]
    d_ff = w_ff1.shape[2] * N_DEV
    n_in = w_in.shape[2] * N_DEV
    gate_off = 3 * sbw + 2 * sgw

    wb = {n: w[n].astype(BF16) for n in BIG}
    full = {n: [None] * depth for n in BIG}
    for n in ("w_in", "w_oa", "w_ob", "w_out"):
        full[n][0] = _all_gather("gather_" + n, wb[n][0])

    def gather(names, l):
        return [_Exchange(wb[n][l], True) for n in names] if l < depth else []

    res = {}

    def update(n, l, parts):
        res[n] = _adam("adam_" + n, parts, w[n], mom[n], var[n], l, res.get(n))

    g_mix3, g_ff3 = g_mix.reshape(depth, 1, d), g_ff.reshape(depth, 1, d)
    gq3, gk3 = g_q.reshape(depth, 1, sbw), g_k.reshape(depth, 1, sbw)
    ln_g3, ln_b3 = sgu_ln_g.reshape(depth, 1, sgw), sgu_ln_b.reshape(depth, 1, sgw)
    b_col = jnp.broadcast_to(b_spatial[..., None], b_spatial.shape + (SGU_LEN,))

    def tile_out(n, dt):
        return lambda tm, tn: [(SDS((s, n), dt), (tm, tn), lambda i, j, k: (i, j))]

    def tile_in(a):
        return lambda tm, tn: [(a, (tm, tn), lambda i, j, k: (i, j))]

    def gate_tile(proj, off, tm, tn):
        return (proj, (tm, tn), lambda i, j, k: (i, off // tn + j))

    saved = []
    cur = xs
    for l in range(depth):
        h = _rms_fwd("rms_mix", cur, g_mix3, l)
        proj, full["w_ff1"][l] = _mm_fwd("proj_in", h, _w_cols, full["w_in"][l], n_in, d, tile_out(n_in, F32),
                                         tn_div=n_in // N_DEV, jobs=gather(["w_ff1"], l))
        qn, kn, vb = _qkv_prep(proj, gq3, gk3, l, sbw)
        o, full["w_ff2"][l] = _attn_fwd(qn, kn, vb, jobs=gather(["w_ff2"], l))
        sg = _sgu_fwd(proj, ln_g3, ln_b3, w_spatial, b_col, l, sbw, sgw)
        ya, yb, merged = _merge_fwd(o, sg, full["w_oa"][l], full["w_ob"][l], proj, d, gate_off)
        x1 = _mm_fwd("proj_out", merged, _w_rows, full["w_out"][l], d, d, tile_out(d, F32),
                     lambda acc, res: (res + acc,), tile_in(cur), k_div=d // N_DEV)[0]
        h2 = _rms_fwd("rms_ff", x1, g_ff3, l)
        a1, r, *nxt = _mm_fwd("ff_up", h2, _w_cols, full["w_ff1"][l], d_ff, d,
                              lambda tm, tn: tile_out(d_ff, F32)(tm, tn) + tile_out(d_ff, BF16)(tm, tn),
                              lambda acc: (acc, jnp.square(jnp.maximum(acc, 0.0))), tn_div=d_ff // N_DEV,
                              jobs=gather(["w_in"], l + 1))
        for n, got in zip(["w_in"], nxt):
            full[n][l + 1] = got
        x2, *nxt = _mm_fwd("ff_down", r, _w_rows, full["w_ff2"][l], d, d_ff, tile_out(d, F32),
                           lambda acc, res: (res + acc,), tile_in(x1), k_div=d_ff // N_DEV,
                           jobs=gather(["w_oa", "w_ob", "w_out"], l + 1))
        for n, got in zip(["w_oa", "w_ob", "w_out"], nxt):
            full[n][l + 1] = got
        saved.append(dict(x=cur, h=h, proj=proj, qn=qn, kn=kn, vb=vb, o=o, sg=sg, ya=ya, yb=yb, merged=merged,
                          x1=x1, h2=h2, a1=a1, r=r))
        cur = x2

    dx, dxb, sq = _loss_head(cur, target)
    loss = lax.psum(0.5 * jnp.sum(sq) / d, ("x", "y", "c"))

    gs = {n: [None] * depth for n in SMALL}
    for l in reversed(range(depth)):
        sv = saved[l]
        gw = _mm_bwd_w("ff_down_dw", sv["r"], dxb, False)
        da1, parts = _mm_bwd_x("ff_down_dx", dxb, _w_rows_t, full["w_ff2"][l], d_ff, d, d, tile_out(d_ff, BF16),
                               lambda acc, a1: (acc * (2.0 * jnp.maximum(a1, 0.0)),), tile_in(sv["a1"]),
                               tn_div=d_ff // N_DEV, jobs=[_Exchange(gw, False)])
        update("w_ff2", l, parts)
        gw = _mm_bwd_w("ff_up_dw", sv["h2"], da1, True)
        dh2, parts = _mm_bwd_x("ff_up_dx", da1, _w_cols_t, full["w_ff1"][l], d, d_ff, d_ff // N_DEV, tile_out(d, F32),
                               jobs=[_Exchange(gw, False)])
        update("w_ff1", l, parts)
        dx1, dx1b, dg = _rms_bwd("rms_ff_bwd", sv["x1"], g_ff3, l, dh2, dx)
        gs["g_ff"][l] = dg.reshape(d)

        gw_out = _mm_bwd_w("proj_out_dw", sv["merged"], dx1b, False)
        dga, dgb, dya, dyb = _mm_bwd_x(
            "proj_out_dx", dx1b, _w_rows_t, full["w_out"][l], d, d, d,
            lambda tm, tn: tile_out(d, BF16)(tm, tn) * 4, _merge_bwd_epilogue,
            lambda tm, tn: [(a, (tm, tn), lambda i, j, k: (i, j)) for a in (sv["ya"], sv["yb"])]
            + [gate_tile(sv["proj"], off, tm, tn) for off in (gate_off, gate_off + d)],
            tn_div=d // N_DEV)
        gw_oa = _mm_bwd_w("oa_dw", sv["o"], dya, True)
        gw_ob = _mm_bwd_w("ob_dw", sv["sg"], dyb, True)
        do = _mm_bwd_x("oa_dx", dya, _w_cols_t, full["w_oa"][l], sbw, d, d // N_DEV, tile_out(sbw, BF16))[0]
        dsg = _mm_bwd_x("ob_dx", dyb, _w_cols_t, full["w_ob"][l], sgw, d, d // N_DEV, tile_out(sgw, F32))[0]

        dqn, dkn, dv, *parts = _attn_bwd(sv["qn"], sv["kn"], sv["vb"], do,
                                         jobs=[_Exchange(g, False) for g in (gw_out, gw_oa, gw_ob)])
        for n, p in zip(("w_out", "w_oa", "w_ob"), parts):
            update(n, l, p)
        dq, dk, dvb, dgq, dgk = _qk_bwd(sv["proj"], gq3, gk3, l, sbw, dqn, dkn, dv)
        gs["g_q"][l], gs["g_k"][l] = dgq.reshape(g_q.shape[1:]), dgk.reshape(g_k.shape[1:])
        du, dvs, dlng, dlnb, dwsp, dbsp = _sgu_bwd(sv["proj"], ln_g3, ln_b3, w_spatial, b_col, l, sbw, sgw, dsg)
        gs["sgu_ln_g"][l], gs["sgu_ln_b"][l] = dlng.reshape(sgw), dlnb.reshape(sgw)
        gs["w_spatial"][l], gs["b_spatial"][l] = dwsp, dbsp[:, :, 0]

        dproj = jnp.concatenate([dq, dk, dvb, du, dvs, dga, dgb], axis=1)
        gw = _mm_bwd_w("proj_in_dw", sv["h"], dproj, True)
        dh, parts = _mm_bwd_x("proj_in_dx", dproj, _w_cols_t, full["w_in"][l], d, n_in, n_in // N_DEV, tile_out(d, F32),
                              jobs=[_Exchange(gw, False)])
        update("w_in", l, parts)
        dx, dxb, dg = _rms_bwd("rms_mix_bwd", sv["x"], g_mix3, l, dh, dx1)
        gs["g_mix"][l] = dg.reshape(d)

    small_parts = _all_gather("gather_small", _pack([jnp.stack(gs[n]) for n in SMALL]))
    packed = _adam_small(small_parts, _pack([w[n] for n in SMALL]), _pack([mom[n] for n in SMALL]),
                         _pack([var[n] for n in SMALL]))
    unpacked = [_unpack(p, [w[n] for n in SMALL]) for p in packed]
    for i, n in enumerate(SMALL):
        res[n] = tuple(u[i] for u in unpacked)

    outs = [loss, dx[None]]
    for which in range(4):
        outs += [res[n][which] for n in ORDER]
    return tuple(outs)
```

```python
import functools

import jax
import jax.numpy as jnp
from jax import lax
from jax.experimental import pallas as pl
from jax.experimental.pallas import tpu as pltpu

F32 = jnp.float32
BF16 = jnp.bfloat16
SDS = jax.ShapeDtypeStruct
MESH = pl.DeviceIdType.MESH

N_DEV = 8
HEAD_DIM = 128
GROUP_DIM = 128
SGU_LEN = 128
SGU_CAUSAL = 64
NORM_EPS = 1e-6
ATT_BLOCK = 128
ATT_DEAD = -110.0
ATT_CHAINS = 4

ADAM_LR, ADAM_B1, ADAM_B2, ADAM_EPS, ADAM_WD, ADAM_STEP = 0.001, 0.9, 0.999, 1e-08, 0.01, 10

ROW_TILE = 512
MM_TM, MM_TN, MM_TK = 1024, 1152, 2304


def _tile(n, target, align=128):
    best = None
    for t in range(align, min(n, target) + 1, align):
        if n % t == 0:
            best = t
    return n if best is None else best


class _Exchange:
    def __init__(self, src, gather):
        self.src, self.gather = src, gather
        self.out = SDS(((N_DEV,) + src.shape) if gather else src.shape, src.dtype)
        self.sems = [pltpu.SemaphoreType.DMA((N_DEV - 1,)), pltpu.SemaphoreType.DMA((N_DEV - 1,)),
                     pltpu.SemaphoreType.DMA(())]

    def plan(self, src_ref, out_ref, send_sems, recv_sems, local_sem):
        x, y, c = _position()
        me = 4 * x + 2 * y + c
        pick = (lambda p: src_ref) if self.gather else (lambda p: src_ref.at[p])
        local = pltpu.make_async_copy(pick(me), out_ref.at[me], local_sem)
        sends, arrivals = [], []
        for k in range(1, N_DEV):
            px = 1 - x if k & 4 else x
            py = 1 - y if k & 2 else y
            pc = 1 - c if k & 1 else c
            peer = 4 * px + 2 * py + pc
            for dst, keep in ((out_ref.at[me], sends), (out_ref.at[peer], arrivals)):
                keep.append(pltpu.make_async_remote_copy(
                    src_ref=pick(peer), dst_ref=dst, send_sem=send_sems.at[k - 1], recv_sem=recv_sems.at[k - 1],
                    device_id=(px, py, pc), device_id_type=MESH))
        return local, sends, arrivals


def _call(name, grid, ins, outs, body, scratch=(), semantics=None, jobs=(), aliases=None):
    n_in, n_out, n_scr, n_job = len(ins), len(outs), len(scratch), len(jobs)
    any_spec = pl.BlockSpec(memory_space=pl.ANY)

    def wrapped(*refs):
        own_in, job_in = refs[:n_in], refs[n_in:n_in + n_job]
        p = n_in + n_job
        own_out, job_out = refs[p:p + n_out], refs[p + n_out:p + n_out + n_job]
        p += n_out + n_job
        own_scr, job_sems = refs[p:p + n_scr], refs[p + n_scr:]
        if not jobs:
            return body(*own_in, *own_out, *own_scr)
        ids = [pl.program_id(a) for a in range(len(grid))]
        first = functools.reduce(jnp.logical_and, [i == 0 for i in ids])
        last = functools.reduce(jnp.logical_and, [i == g - 1 for i, g in zip(ids, grid)])
        plans = [jb.plan(job_in[n], job_out[n], *job_sems[3 * n:3 * n + 3]) for n, jb in enumerate(jobs)]

        @pl.when(first)
        def _():
            for local, sends, _ in plans:
                local.start()
                for cp in sends:
                    cp.start()

        body(*own_in, *own_out, *own_scr)

        @pl.when(last)
        def _():
            for local, sends, arrivals in plans:
                for cp in arrivals:
                    cp.wait_recv()
                for cp in sends:
                    cp.wait_send()
                local.wait()

    spec = lambda blk, im: any_spec if blk is None else pl.BlockSpec(blk, im)
    res = pl.pallas_call(
        wrapped, name=name, grid=grid,
        in_specs=[spec(blk, im) for _, blk, im in ins] + [any_spec] * n_job,
        out_specs=[spec(blk, im) for _, blk, im in outs] + [any_spec] * n_job,
        out_shape=[s for s, _, _ in outs] + [jb.out for jb in jobs],
        scratch_shapes=list(scratch) + [sem for jb in jobs for sem in jb.sems],
        input_output_aliases=aliases or {},
        compiler_params=pltpu.CompilerParams(
            dimension_semantics=("arbitrary",) * len(grid) if jobs or not semantics else semantics),
    )(*[a for a, _, _ in ins], *[jb.src for jb in jobs])
    return res


def _dot(a, b, dims=((1,), (0,))):
    return lax.dot_general(a, b, (dims, ((), ())), preferred_element_type=F32)


NN = ((1,), (0,))
NT = ((1,), (1,))
TN = ((0,), (0,))


def _matmul(name, grid, a, b, dims, outs, epilogue=None, extras=(), jobs=()):
    nk = grid[2]
    n_ex, n_out = len(extras), len(outs)
    acc_shape = tuple(d for d in outs[0][1] if d is not None)

    def body(*refs):
        a_ref, b_ref = refs[0], refs[1]
        ex_refs = refs[2:2 + n_ex]
        out_refs = refs[2 + n_ex:2 + n_ex + n_out]
        def product():
            return _dot(a_ref[...].astype(BF16), b_ref[...].astype(BF16), dims)

        def finish(acc):
            vals = (acc,) if epilogue is None else epilogue(acc, *[r[...] for r in ex_refs])
            for r, v in zip(out_refs, vals):
                r[...] = v.astype(r.dtype)

        if nk == 1:
            finish(product())
        else:
            acc_ref = refs[-1]
            k = pl.program_id(2)

            @pl.when(k == 0)
            def _():
                acc_ref[...] = jnp.zeros_like(acc_ref)

            @pl.when(k < nk - 1)
            def _():
                acc_ref[...] += product()

            @pl.when(k == nk - 1)
            def _():
                finish(acc_ref[...] + product())

    return _call(name, grid, [a, b, *extras], list(outs), body,
                 scratch=[pltpu.VMEM(acc_shape, F32)] if nk > 1 else [],
                 semantics=("parallel", "parallel", "arbitrary"), jobs=jobs)


def _whole(gathered, col_sharded):
    if col_sharded:
        return jnp.transpose(gathered, (1, 0, 2)).reshape(gathered.shape[1], -1)
    return gathered.reshape(-1, gathered.shape[2])


def _n_tile(n, target):
    return _tile(n, MM_TN if n % MM_TN == 0 else target)


def _mm_fwd(name, a, w, outs_fn, epilogue=None, extras_fn=None, tn=1024, jobs=()):
    m, k_dim = a.shape
    n_out = w.shape[1]
    tm, tn, tk = _tile(m, MM_TM, 8), _n_tile(n_out, tn), _tile(k_dim, MM_TK)
    grid = (m // tm, n_out // tn, k_dim // tk)
    extras = extras_fn(tm, tn) if extras_fn else ()
    return _matmul(name, grid, (a, (tm, tk), lambda i, j, k: (i, k)), (w, (tk, tn), lambda i, j, k: (k, j)), NN,
                   outs_fn(tm, tn), epilogue, extras, jobs)


def _mm_bwd_x(name, a, w, outs_fn, epilogue=None, extras_fn=None, tn=1024, jobs=()):
    m, k_dim = a.shape
    n_out = w.shape[0]
    tm, tn, tk = _tile(m, MM_TM, 8), _tile(n_out, tn), _tile(k_dim, MM_TK)
    grid = (m // tm, n_out // tn, k_dim // tk)
    extras = extras_fn(tm, tn) if extras_fn else ()
    return _matmul(name, grid, (a, (tm, tk), lambda i, j, k: (i, k)), (w, (tn, tk), lambda i, j, k: (j, k)), NT,
                   outs_fn(tm, tn), epilogue, extras, jobs)


def _mm_bwd_w(name, a, b, col_sharded):
    m, ka = a.shape
    n = b.shape[1]
    ts = _tile(m, 2 * MM_TM, 8)
    shard = (n if col_sharded else ka) // N_DEV
    a_spec = lambda ti: (a, (ts, ti), lambda i, j, k: (k, i))
    b_spec = lambda tj: (b, (ts, tj), lambda i, j, k: (k, j))
    if col_sharded and shard % 1024 and shard % MM_TN:
        ti, tj = _tile(ka, 1024), _tile(n, 1024)
        out = (SDS((ka, n), BF16), (ti, tj), lambda i, j, k: (i, j))
        whole = _matmul(name, (ka // ti, n // tj, m // ts), a_spec(ti), b_spec(tj), TN, [out])[0]
        return jnp.transpose(whole.reshape(ka, N_DEV, shard), (1, 0, 2))
    if col_sharded:
        ti, tj = _tile(ka, 1024), _n_tile(shard, 1024)
        nps = shard // tj
        out = (SDS((N_DEV, ka, shard), BF16), (None, ti, tj), lambda i, j, k: (j // nps, i, j % nps))
    else:
        ti, tj = _tile(shard, 1024), _tile(n, 1024)
        ips = shard // ti
        out = (SDS((N_DEV, shard, n), BF16), (None, ti, tj), lambda i, j, k: (i // ips, i % ips, j))
    return _matmul(name, (ka // ti, n // tj, m // ts), a_spec(ti), b_spec(tj), TN, [out])[0]


def _rms_fwd(name, x, g3, l):
    s, d = x.shape
    ts = _tile(s, ROW_TILE, 8)

    def body(x_ref, g_ref, h_ref):
        xv = x_ref[...]
        r = lax.rsqrt(jnp.mean(xv * xv, axis=-1, keepdims=True) + NORM_EPS)
        h_ref[...] = (xv * r * g_ref[...]).astype(BF16)

    return _call(name, (s // ts,), [(x, (ts, d), lambda i: (i, 0)), (g3, (None, 1, d), lambda i: (l, 0, 0))],
                 [(SDS((s, d), BF16), (ts, d), lambda i: (i, 0))], body, semantics=("parallel",))[0]


def _rms_bwd(name, x, g3, l, dh, dres):
    s, d = x.shape
    ts = _tile(s, ROW_TILE, 8)

    def body(x_ref, g_ref, dh_ref, dres_ref, dx_ref, dxb_ref, dg_ref):
        xv = x_ref[...]
        r = lax.rsqrt(jnp.mean(xv * xv, axis=-1, keepdims=True) + NORM_EPS)
        xhat = xv * r
        dhv = dh_ref[...]
        gy = dhv * g_ref[...]
        m = jnp.mean(gy * xhat, axis=-1, keepdims=True)
        dx = dres_ref[...] + r * (gy - xhat * m)
        dx_ref[...] = dx
        dxb_ref[...] = dx.astype(BF16)
        part = jnp.sum(dhv * xhat, axis=0, keepdims=True)

        @pl.when(pl.program_id(0) == 0)
        def _():
            dg_ref[...] = part

        @pl.when(pl.program_id(0) > 0)
        def _():
            dg_ref[...] += part

    row = lambda i: (i, 0)
    return _call(name, (s // ts,),
                 [(x, (ts, d), row), (g3, (None, 1, d), lambda i: (l, 0, 0)), (dh, (ts, d), row), (dres, (ts, d), row)],
                 [(SDS((s, d), F32), (ts, d), row), (SDS((s, d), BF16), (ts, d), row),
                  (SDS((1, d), F32), (1, d), lambda i: (0, 0))], body)


def _loss_head(y, target):
    s, d = y.shape
    ts = _tile(s, ROW_TILE, 8)

    def body(y_ref, t_ref, dy_ref, dyb_ref, sq_ref):
        diff = y_ref[...] - t_ref[...]
        dy = diff / d
        dy_ref[...] = dy
        dyb_ref[...] = dy.astype(BF16)
        part = jnp.sum(diff * diff, axis=0, keepdims=True)

        @pl.when(pl.program_id(0) == 0)
        def _():
            sq_ref[...] = part

        @pl.when(pl.program_id(0) > 0)
        def _():
            sq_ref[...] += part

    row = lambda i: (i, 0)
    return _call("loss_head", (s // ts,), [(y, (ts, d), row), (target, (ts, d), row)],
                 [(SDS((s, d), F32), (ts, d), row), (SDS((s, d), BF16), (ts, d), row),
                  (SDS((1, d), F32), (1, d), lambda i: (0, 0))], body)


def _qkv_prep(proj, gq3, gk3, l, sbw):
    s = proj.shape[0]
    ts = _tile(s, ROW_TILE, 8)
    heads = sbw // HEAD_DIM

    def body(q_ref, k_ref, v_ref, gq_ref, gk_ref, qn_ref, kn_ref, vb_ref):
        for h in range(heads):
            sl = slice(h * HEAD_DIM, (h + 1) * HEAD_DIM)
            for src, g_ref, dst in ((q_ref, gq_ref, qn_ref), (k_ref, gk_ref, kn_ref)):
                t = src[:, sl]
                r = lax.rsqrt(jnp.mean(t * t, axis=-1, keepdims=True) + NORM_EPS)
                dst[:, sl] = (t * r * g_ref[:, sl]).astype(BF16)
        vb_ref[...] = v_ref[...].astype(BF16)

    gspec = lambda g: (g, (None, 1, sbw), lambda i: (l, 0, 0))
    col = lambda c: (proj, (ts, sbw), lambda i: (i, c))
    out = (SDS((s, sbw), BF16), (ts, sbw), lambda i: (i, 0))
    return _call("qkv_prep", (s // ts,), [col(0), col(1), col(2), gspec(gq3), gspec(gk3)], [out, out, out], body,
                 semantics=("parallel",))


def _qk_bwd(proj, gq3, gk3, l, sbw, dqn, dkn, dv):
    s = proj.shape[0]
    ts = _tile(s, ROW_TILE, 8)
    heads = sbw // HEAD_DIM

    def body(q_ref, k_ref, gq_ref, gk_ref, dqn_ref, dkn_ref, dv_ref, dq_ref, dk_ref, dvb_ref, dgq_ref, dgk_ref):
        first = pl.program_id(0) == 0
        for src, g_ref, dn_ref, dst, dg_ref in ((q_ref, gq_ref, dqn_ref, dq_ref, dgq_ref),
                                                (k_ref, gk_ref, dkn_ref, dk_ref, dgk_ref)):
            for h in range(heads):
                sl = slice(h * HEAD_DIM, (h + 1) * HEAD_DIM)
                t = src[:, sl]
                r = lax.rsqrt(jnp.mean(t * t, axis=-1, keepdims=True) + NORM_EPS)
                xhat = t * r
                dn = dn_ref[:, sl]
                gy = dn * g_ref[:, sl]
                m = jnp.mean(gy * xhat, axis=-1, keepdims=True)
                dst[:, sl] = (r * (gy - xhat * m)).astype(BF16)
                part = jnp.sum(dn * xhat, axis=0, keepdims=True)

                @pl.when(first)
                def _():
                    dg_ref[:, sl] = part

                @pl.when(jnp.logical_not(first))
                def _():
                    dg_ref[:, sl] += part
        dvb_ref[...] = dv_ref[...].astype(BF16)

    gspec = lambda g: (g, (None, 1, sbw), lambda i: (l, 0, 0))
    col = lambda c: (proj, (ts, sbw), lambda i: (i, c))
    row = lambda a: (a, (ts, sbw), lambda i: (i, 0))
    outb = (SDS((s, sbw), BF16), (ts, sbw), lambda i: (i, 0))
    outg = (SDS((1, sbw), F32), (1, sbw), lambda i: (0, 0))
    return _call("qk_bwd", (s // ts,), [col(0), col(1), gspec(gq3), gspec(gk3), row(dqn), row(dkn), row(dv)],
                 [outb, outb, outb, outg, outg], body)


def _softplus(z):
    return jnp.maximum(z, 0.0) + jnp.log1p(jnp.exp(-jnp.abs(z)))


def _tri_dot(x, tri):
    hi = x.astype(BF16)
    lo = (x - hi.astype(F32)).astype(BF16)
    return _dot(hi, tri) + _dot(lo, tri)


def _att_common(tb):
    rows = lax.broadcasted_iota(jnp.int32, (tb, tb), 0)
    cols = lax.broadcasted_iota(jnp.int32, (tb, tb), 1)
    suffix = (rows >= cols).astype(BF16)
    prefix = (rows <= cols).astype(BF16)
    return rows, cols, suffix, prefix, jnp.ones((tb, tb), BF16)


def _key_norm_max(k_ref, kmax_ref):
    kf = k_ref[...].astype(F32)
    n2 = jnp.sum(kf * kf, axis=1, keepdims=True)
    kmax_ref[...] = jnp.broadcast_to(jnp.max(n2, axis=0, keepdims=True), kmax_ref.shape)


def _att_reach(q, kmax_ref, scale):
    qf = q.astype(F32)
    qn2 = jnp.sum(qf * qf, axis=1, keepdims=True)
    return scale * jnp.sqrt(qn2 * kmax_ref[0:1, 0:1]) * 1.001 + 1e-3


def _att_chains(s, tb):
    return ATT_CHAINS if s % (ATT_CHAINS * tb) == 0 else 1


def _att_block(diag, j, tb, rows, cols):
    kb = diag - j
    off = pl.multiple_of(jnp.maximum(kb, 0) * tb, tb)
    limit = jnp.where(kb >= 0, diag * tb, -(1 << 30))
    return off, (cols + kb * tb) < (rows + limit)


def _attn_fwd(qn, kn, vb, jobs=()):
    s, sbw = qn.shape
    heads = sbw // HEAD_DIM
    tb = _tile(s, ATT_BLOCK, 8)
    nch = _att_chains(s, tb)
    tq = nch * tb
    scale = HEAD_DIM ** -0.5

    def body(q_ref, k_ref, v_ref, o_ref, kmax_ref, reach_ref, run_ref, acc_ref):
        qi = pl.program_id(1)

        @pl.when(qi == 0)
        def _():
            _key_norm_max(k_ref, kmax_ref)

        rows, cols, suffix, _, ones = _att_common(tb)
        tri = jnp.concatenate([suffix, ones], axis=1)
        diag = [qi * nch + c for c in range(nch)]
        for c in range(nch):
            q = q_ref[c * tb:(c + 1) * tb, :]
            reach_ref[c] = jnp.broadcast_to(_att_reach(q, kmax_ref, scale), (tb, tb))
        run_ref[...] = jnp.zeros_like(run_ref)
        acc_ref[...] = jnp.zeros_like(acc_ref)

        def cond(carry):
            return jnp.logical_and(carry[0] <= diag[-1], carry[1])

        def step(carry):
            j, alive, chains = carry[0], None, range(nch)
            offs, pasts = zip(*[_att_block(diag[c], j, tb, rows, cols) for c in chains])
            runs = [run_ref[c] for c in chains]
            zs = [_dot(q_ref[c * tb:(c + 1) * tb, :], k_ref[pl.ds(offs[c], tb), :], NT) * scale for c in chains]
            keeps = [jnp.where(pasts[c], -_softplus(zs[c]), 0.0) for c in chains]
            sums = [_tri_dot(keeps[c], tri) for c in chains]
            probs = [jnp.where(pasts[c], jnp.exp(zs[c] + sums[c][:, :tb] + runs[c]), 0.0).astype(BF16) for c in chains]
            outs = [_dot(probs[c], v_ref[pl.ds(offs[c], tb), :]) for c in chains]
            for c in chains:
                acc_ref[c] += outs[c]
                run = runs[c] + sums[c][:, tb:]
                run_ref[c] = run
                m = jnp.max(run + reach_ref[c])
                alive = m if alive is None else jnp.maximum(alive, m)
            return j + 1, alive > ATT_DEAD

        lax.while_loop(cond, step, (jnp.int32(0), jnp.bool_(True)))
        for c in range(nch):
            o_ref[c * tb:(c + 1) * tb, :] = acc_ref[c].astype(o_ref.dtype)

    qspec = lambda a: (a, (tq, HEAD_DIM), lambda h, i: (i, h))
    full = lambda a: (a, (s, HEAD_DIM), lambda h, i: (0, h))
    state = pltpu.VMEM((nch, tb, tb), F32)
    return _call("attn_fwd", (heads, s // tq), [qspec(qn), full(kn), full(vb)],
                 [(SDS((s, sbw), BF16), (tq, HEAD_DIM), lambda h, i: (i, h))], body,
                 scratch=[pltpu.VMEM((8, 128), F32), state, state, pltpu.VMEM((nch, tb, HEAD_DIM), F32)],
                 semantics=("parallel", "arbitrary"), jobs=jobs)


def _attn_bwd(qn, kn, vb, do, jobs=()):
    s, sbw = qn.shape
    heads = sbw // HEAD_DIM
    tb = _tile(s, ATT_BLOCK, 8)
    nch = _att_chains(s, tb)
    tq = nch * tb
    scale = HEAD_DIM ** -0.5

    def body(q_ref, k_ref, v_ref, do_ref, dq_ref, dk_ref, dv_ref, kmax_ref, reach_ref, run_ref, g_ref):
        qi = pl.program_id(1)

        @pl.when(qi == 0)
        def _():
            _key_norm_max(k_ref, kmax_ref)
            dk_ref[...] = jnp.zeros_like(dk_ref)
            dv_ref[...] = jnp.zeros_like(dv_ref)

        rows, cols, suffix, prefix, ones = _att_common(tb)
        tri_back = jnp.concatenate([suffix, ones], axis=1)
        tri_fwd = jnp.concatenate([prefix, ones], axis=1)
        diag = [qi * nch + c for c in range(nch)]
        rows_of = lambda ref, c: ref[c * tb:(c + 1) * tb, :]
        for c in range(nch):
            reach_ref[c] = jnp.broadcast_to(_att_reach(rows_of(q_ref, c), kmax_ref, scale), (tb, tb))
        run_ref[...] = jnp.zeros_like(run_ref)
        dq_ref[...] = jnp.zeros_like(dq_ref)

        def cond(carry):
            return jnp.logical_and(carry[0] <= diag[-1], carry[1])

        chains = range(nch)

        def sweep_back(carry):
            j, alive = carry[0], None
            offs, pasts = zip(*[_att_block(diag[c], j, tb, rows, cols) for c in chains])
            runs = [run_ref[c] for c in chains]
            zs = [_dot(rows_of(q_ref, c), k_ref[pl.ds(offs[c], tb), :], NT) * scale for c in chains]
            das = [_dot(rows_of(do_ref, c), v_ref[pl.ds(offs[c], tb), :], NT) for c in chains]
            keeps = [jnp.where(pasts[c], -_softplus(zs[c]), 0.0) for c in chains]
            sums = [_tri_dot(keeps[c], tri_back) for c in chains]
            probs = [jnp.where(pasts[c], jnp.exp(zs[c] + sums[c][:, :tb] + runs[c]), 0.0) for c in chains]
            dvs = [_dot(probs[c].astype(BF16), rows_of(do_ref, c), TN) for c in chains]
            for c in chains:
                g_ref[c, j] = probs[c] * das[c]
                dv_ref[pl.ds(offs[c], tb), :] += dvs[c]
                run = runs[c] + sums[c][:, tb:]
                run_ref[c] = run
                m = jnp.max(run + reach_ref[c])
                alive = m if alive is None else jnp.maximum(alive, m)
            return j + 1, alive > ATT_DEAD

        nsteps = lax.while_loop(cond, sweep_back, (jnp.int32(0), jnp.bool_(True)))[0]
        run_ref[...] = jnp.zeros_like(run_ref)

        def sweep_fwd(i, carry):
            j = nsteps - 1 - i
            offs, pasts = zip(*[_att_block(diag[c], j, tb, rows, cols) for c in chains])
            runs = [run_ref[c] for c in chains]
            ks = [k_ref[pl.ds(offs[c], tb), :] for c in chains]
            sgs = [jax.nn.sigmoid(_dot(rows_of(q_ref, c), ks[c], NT) * scale) for c in chains]
            gs = [g_ref[c, j] for c in chains]
            sums = [_tri_dot(gs[c], tri_fwd) for c in chains]
            dzs = [(jnp.where(pasts[c], gs[c] - sgs[c] * (runs[c] + sums[c][:, :tb]), 0.0) * scale).astype(BF16)
                   for c in chains]
            dqs = [_dot(dzs[c], ks[c]) for c in chains]
            dks = [_dot(dzs[c], rows_of(q_ref, c), TN) for c in chains]
            for c in chains:
                dq_ref[c * tb:(c + 1) * tb, :] += dqs[c]
                dk_ref[pl.ds(offs[c], tb), :] += dks[c]
                run_ref[c] = runs[c] + sums[c][:, tb:]
            return carry

        lax.fori_loop(0, nsteps, sweep_fwd, 0)

    qspec = lambda a: (a, (tq, HEAD_DIM), lambda h, i: (i, h))
    full = lambda a: (a, (s, HEAD_DIM), lambda h, i: (0, h))
    outq = (SDS((s, sbw), F32), (tq, HEAD_DIM), lambda h, i: (i, h))
    outf = (SDS((s, sbw), F32), (s, HEAD_DIM), lambda h, i: (0, h))
    state = pltpu.VMEM((nch, tb, tb), F32)
    return _call("attn_bwd", (heads, s // tq), [qspec(qn), full(kn), full(vb), qspec(do)], [outq, outf, outf], body,
                 scratch=[pltpu.VMEM((8, 128), F32), state, state, pltpu.VMEM((nch, s // tb, tb, tb), F32)],
                 semantics=("parallel", "arbitrary"), jobs=jobs)


_SQRT_HALF = 0.7071067811865476
_INV_SQRT_2PI = 0.3989422804014327


def _gelu(x):
    return 0.5 * x * (1.0 + lax.erf(x * _SQRT_HALF))


def _gelu_grad(x):
    return 0.5 * (1.0 + lax.erf(x * _SQRT_HALF)) + x * jnp.exp(-0.5 * x * x) * _INV_SQRT_2PI


def _sgu_mask():
    i = lax.broadcasted_iota(jnp.int32, (SGU_LEN, SGU_LEN), 0) // SGU_CAUSAL
    j = lax.broadcasted_iota(jnp.int32, (SGU_LEN, SGU_LEN), 1) // SGU_CAUSAL
    return j <= i


def _sgu_norm(vg, lng, lnb):
    mu = jnp.mean(vg, axis=-1, keepdims=True)
    xc = vg - mu
    rstd = lax.rsqrt(jnp.mean(xc * xc, axis=-1, keepdims=True) + NORM_EPS)
    xhat = xc * rstd
    return xhat, rstd, xhat * lng + lnb


def _sgu_specs(proj, ln_g3, ln_b3, w_sp, b_col, l, sbw, sgw, rb):
    groups = sgw // GROUP_DIM
    ucol, vcol = 3 * sbw // sgw, 3 * sbw // sgw + 1
    vec = lambda a: (a, (None, 1, sgw), lambda i: (l, 0, 0))
    mat = lambda a: (a, (None, groups, SGU_LEN, SGU_LEN), lambda i: (l, 0, 0, 0))
    return [(proj, (rb, sgw), lambda i: (i, ucol)), (proj, (rb, sgw), lambda i: (i, vcol)),
            vec(ln_g3), vec(ln_b3), mat(w_sp), mat(b_col)]


def _sgu_fwd(proj, ln_g3, ln_b3, w_sp, b_col, l, sbw, sgw):
    s = proj.shape[0]
    rb = _tile(s, 2 * SGU_LEN, SGU_LEN)
    groups = sgw // GROUP_DIM

    def body(u_ref, v_ref, lng_ref, lnb_ref, w_ref, b_ref, s_ref):
        mask = _sgu_mask()
        _, _, vln = _sgu_norm(_gelu(v_ref[...]), lng_ref[...], lnb_ref[...])
        vb = vln.astype(BF16)
        for g in range(groups):
            cs = slice(g * GROUP_DIM, (g + 1) * GROUP_DIM)
            wm = jnp.where(mask, w_ref[g], 0.0).astype(BF16)
            for c in range(rb // SGU_LEN):
                rs = slice(c * SGU_LEN, (c + 1) * SGU_LEN)
                mixed = _dot(wm, vb[rs, cs]) + b_ref[g]
                s_ref[rs, cs] = (_gelu(u_ref[rs, cs]) * mixed).astype(BF16)

    return _call("sgu_fwd", (s // rb,), _sgu_specs(proj, ln_g3, ln_b3, w_sp, b_col, l, sbw, sgw, rb),
                 [(SDS((s, sgw), BF16), (rb, sgw), lambda i: (i, 0))], body, semantics=("parallel",))[0]


def _sgu_bwd(proj, ln_g3, ln_b3, w_sp, b_col, l, sbw, sgw, ds):
    s = proj.shape[0]
    rb = _tile(s, 2 * SGU_LEN, SGU_LEN)
    groups = sgw // GROUP_DIM
    nsteps = s // rb

    def body(u_ref, v_ref, lng_ref, lnb_ref, w_ref, b_ref, ds_ref,
             du_ref, dv_ref, dlng_ref, dlnb_ref, dw_ref, db_ref, dvln_ref, dw_acc, db_acc):
        step = pl.program_id(0)

        @pl.when(step == 0)
        def _():
            dw_acc[...] = jnp.zeros_like(dw_acc)
            db_acc[...] = jnp.zeros_like(db_acc)

        mask = _sgu_mask()
        vp = v_ref[...]
        lng = lng_ref[...]
        xhat, rstd, vln = _sgu_norm(_gelu(vp), lng, lnb_ref[...])
        vb = vln.astype(BF16)
        for g in range(groups):
            cs = slice(g * GROUP_DIM, (g + 1) * GROUP_DIM)
            wm = jnp.where(mask, w_ref[g], 0.0).astype(BF16)
            for c in range(rb // SGU_LEN):
                rs = slice(c * SGU_LEN, (c + 1) * SGU_LEN)
                vbp = vb[rs, cs]
                mixed = _dot(wm, vbp) + b_ref[g]
                up = u_ref[rs, cs]
                dsp = ds_ref[rs, cs]
                du_ref[rs, cs] = (dsp * mixed * _gelu_grad(up)).astype(BF16)
                dmixed = dsp * _gelu(up)
                dmb = dmixed.astype(BF16)
                dvln_ref[rs, cs] = _dot(wm, dmb, TN)
                dw_acc[g] += _dot(dmb, vbp, NT)
                db_acc[g] += dmixed
        dvln = dvln_ref[...]
        part_g = jnp.sum(dvln * xhat, axis=0, keepdims=True)
        part_b = jnp.sum(dvln, axis=0, keepdims=True)

        @pl.when(step == 0)
        def _():
            dlng_ref[...] = part_g
            dlnb_ref[...] = part_b

        @pl.when(step > 0)
        def _():
            dlng_ref[...] += part_g
            dlnb_ref[...] += part_b

        dxhat = dvln * lng
        m1 = jnp.mean(dxhat, axis=-1, keepdims=True)
        m2 = jnp.mean(dxhat * xhat, axis=-1, keepdims=True)
        dv_ref[...] = (rstd * (dxhat - m1 - xhat * m2) * _gelu_grad(vp)).astype(BF16)

        @pl.when(step == nsteps - 1)
        def _():
            for g in range(groups):
                dw_ref[g] = jnp.where(mask, dw_acc[g], 0.0)
                db_ref[g] = jnp.broadcast_to(jnp.sum(db_acc[g], axis=-1, keepdims=True), (SGU_LEN, SGU_LEN))

    row = lambda i: (i, 0)
    outb = (SDS((s, sgw), BF16), (rb, sgw), row)
    outv = (SDS((1, sgw), F32), (1, sgw), lambda i: (0, 0))
    outm = (SDS((groups, SGU_LEN, SGU_LEN), F32), (groups, SGU_LEN, SGU_LEN), lambda i: (0, 0, 0))
    acc = pltpu.VMEM((groups, SGU_LEN, SGU_LEN), F32)
    return _call("sgu_bwd", (nsteps,),
                 _sgu_specs(proj, ln_g3, ln_b3, w_sp, b_col, l, sbw, sgw, rb) + [(ds, (rb, sgw), row)],
                 [outb, outb, outv, outv, outm, outm], body, scratch=[pltpu.VMEM((rb, sgw), F32), acc, acc])


def _merge_fwd(o, sg, w_oa, w_ob, proj, d, gate_off):
    s, kw = o.shape
    tm, tn = _tile(s, MM_TM, 8), _tile(d, 512)

    def body(o_ref, s_ref, wa_ref, wb_ref, ga_ref, gb_ref, ya_ref, yb_ref, mg_ref):
        ya = _dot(o_ref[...], wa_ref[...])
        yb = _dot(s_ref[...], wb_ref[...])
        ya_ref[...] = ya
        yb_ref[...] = yb
        mg_ref[...] = (jax.nn.sigmoid(ga_ref[...]) * ya + jax.nn.sigmoid(gb_ref[...]) * yb).astype(BF16)

    act = lambda a: (a, (tm, kw), lambda i, j: (i, 0))
    wsp = lambda w: (w, (kw, tn), lambda i, j: (0, j))
    gate = lambda off: (proj, (tm, tn), lambda i, j: (i, off // tn + j))
    tile = lambda dt: (SDS((s, d), dt), (tm, tn), lambda i, j: (i, j))
    return _call("merge_fwd", (s // tm, d // tn), [act(o), act(sg), wsp(w_oa), wsp(w_ob), gate(gate_off), gate(gate_off + d)],
                 [tile(F32), tile(F32), tile(BF16)], body, semantics=("parallel", "parallel"))


def _merge_bwd_epilogue(dm, ya, yb, ga, gb):
    sa, sb = jax.nn.sigmoid(ga), jax.nn.sigmoid(gb)
    return dm * ya * sa * (1.0 - sa), dm * yb * sb * (1.0 - sb), dm * sa, dm * sb


def _position():
    x, y, c = lax.axis_index("x"), lax.axis_index("y"), lax.axis_index("c")
    return x, y, c


def _all_gather(name, shard):
    def body(x_ref, out_ref, send_sems, recv_sems, local_sem):
        x, y, c = _position()
        me, sibling = (x, y, c), (x, y, 1 - c)
        chips = [(1 - x, y), (x, 1 - y), (1 - x, 1 - y)]

        def slot(px, py, pc):
            return out_ref.at[4 * px + 2 * py + pc]

        def copy(k, block, to, src=None):
            return pltpu.make_async_remote_copy(
                src_ref=slot(*block) if src is None else src, dst_ref=slot(*block),
                send_sem=send_sems.at[k], recv_sem=recv_sems.at[k], device_id=to, device_id_type=MESH)

        mine = pltpu.make_async_copy(x_ref, slot(*me), local_sem)
        mine.start()
        first = [copy(0, me, sibling, src=x_ref)]
        first += [copy(1 + j, me, (*chip, c), src=x_ref) for j, chip in enumerate(chips)]
        for cp in first:
            cp.start()
        passed = [copy(4 + j, (*chip, c), sibling) for j, chip in enumerate(chips)]
        for j, chip in enumerate(chips):
            copy(1 + j, (*chip, c), me).wait_recv()
            passed[j].start()
        copy(0, sibling, me).wait_recv()
        for j, chip in enumerate(chips):
            copy(4 + j, (*chip, 1 - c), me).wait_recv()
        for cp in first + passed:
            cp.wait_send()
        mine.wait()

    return pl.pallas_call(
        body, name=name, out_shape=SDS((N_DEV,) + shard.shape, shard.dtype),
        in_specs=[pl.BlockSpec(memory_space=pl.ANY)], out_specs=pl.BlockSpec(memory_space=pl.ANY),
        scratch_shapes=[pltpu.SemaphoreType.DMA((7,)), pltpu.SemaphoreType.DMA((7,)), pltpu.SemaphoreType.DMA(())],
    )(shard)


def _adam_math(g, w, m, v):
    m = ADAM_B1 * m + (1.0 - ADAM_B1) * g
    v = ADAM_B2 * v + (1.0 - ADAM_B2) * (g * g)
    m_hat = m / (1.0 - ADAM_B1 ** ADAM_STEP)
    v_hat = v / (1.0 - ADAM_B2 ** ADAM_STEP)
    delta = -ADAM_LR * (m_hat / (jnp.sqrt(v_hat) + ADAM_EPS) + ADAM_WD * w)
    return delta, m, v


def _adam(name, parts, w, m, v, l, prev):
    _, r, c = w.shape
    tr, tc = _tile(r, 256, 8), _tile(c, MM_TN)

    def body(p_ref, w_ref, m_ref, v_ref, *rest):
        g_ref, d_ref, mo_ref, vo_ref = rest[-4:]
        g = p_ref[0].astype(F32)
        for q in range(1, N_DEV):
            g = g + p_ref[q].astype(F32)
        delta, mn, vn = _adam_math(g, w_ref[...], m_ref[...], v_ref[...])
        g_ref[...] = g
        d_ref[...] = delta
        mo_ref[...] = mn
        vo_ref[...] = vn

    cur = lambda a: (a, (None, tr, tc), lambda i, j: (l, i, j))
    out = (SDS(w.shape, F32), (None, tr, tc), lambda i, j: (l, i, j))
    ins = [(parts, (N_DEV, tr, tc), lambda i, j: (0, i, j)), cur(w), cur(m), cur(v)]
    aliases = None
    if prev is not None:
        ins += [(p, None, None) for p in prev]
        aliases = {4 + n: n for n in range(4)}
    return _call(name, (r // tr, c // tc), ins, [out, out, out, out], body, semantics=("parallel", "parallel"),
                 aliases=aliases)


def _adam_small(parts, w, m, v):
    r = w.shape[0]
    tr = _tile(r, 1096, 8)

    def body(p_ref, w_ref, m_ref, v_ref, g_ref, d_ref, mo_ref, vo_ref):
        g = p_ref[0]
        for q in range(1, N_DEV):
            g = g + p_ref[q]
        delta, mn, vn = _adam_math(g, w_ref[...], m_ref[...], v_ref[...])
        g_ref[...] = g
        d_ref[...] = delta
        mo_ref[...] = mn
        vo_ref[...] = vn

    cur = lambda a: (a, (tr, 128), lambda i: (i, 0))
    out = (SDS(w.shape, F32), (tr, 128), lambda i: (i, 0))
    return _call("adam_small", (r // tr,), [(parts, (N_DEV, tr, 128), lambda i: (0, i, 0)), cur(w), cur(m), cur(v)],
                 [out, out, out, out], body, semantics=("parallel",))


SMALL = ("g_mix", "g_q", "g_k", "sgu_ln_g", "sgu_ln_b", "w_spatial", "b_spatial", "g_ff")
BIG = ("w_in", "w_oa", "w_ob", "w_out", "w_ff1", "w_ff2")
COL_SHARDED = ("w_in", "w_oa", "w_ob", "w_ff1")
ORDER = ("g_mix", "w_in", "g_q", "g_k", "sgu_ln_g", "sgu_ln_b", "w_spatial", "b_spatial", "w_oa", "w_ob", "w_out",
         "g_ff", "w_ff1", "w_ff2")


def _pack(arrs):
    return jnp.concatenate([a.reshape(-1) for a in arrs]).reshape(-1, 128)


def _unpack(packed, like):
    flat, out, off = packed.reshape(-1), [], 0
    for a in like:
        out.append(flat[off:off + a.size].reshape(a.shape))
        off += a.size
    return out


def kernel(x, g_mix, w_in, g_q, g_k, sgu_ln_g, sgu_ln_b, w_spatial, b_spatial, w_oa, w_ob, w_out, g_ff, w_ff1, w_ff2, loss_target, m_g_mix, m_w_in, m_g_q, m_g_k, m_sgu_ln_g, m_sgu_ln_b, m_w_spatial, m_b_spatial, m_w_oa, m_w_ob, m_w_out, m_g_ff, m_w_ff1, m_w_ff2, v_g_mix, v_w_in, v_g_q, v_g_k, v_sgu_ln_g, v_sgu_ln_b, v_w_spatial, v_b_spatial, v_w_oa, v_w_ob, v_w_out, v_g_ff, v_w_ff1, v_w_ff2):
    w = dict(g_mix=g_mix, w_in=w_in, g_q=g_q, g_k=g_k, sgu_ln_g=sgu_ln_g, sgu_ln_b=sgu_ln_b, w_spatial=w_spatial,
             b_spatial=b_spatial, w_oa=w_oa, w_ob=w_ob, w_out=w_out, g_ff=g_ff, w_ff1=w_ff1, w_ff2=w_ff2)
    mom = dict(g_mix=m_g_mix, w_in=m_w_in, g_q=m_g_q, g_k=m_g_k, sgu_ln_g=m_sgu_ln_g, sgu_ln_b=m_sgu_ln_b,
               w_spatial=m_w_spatial, b_spatial=m_b_spatial, w_oa=m_w_oa, w_ob=m_w_ob, w_out=m_w_out, g_ff=m_g_ff,
               w_ff1=m_w_ff1, w_ff2=m_w_ff2)
    var = dict(g_mix=v_g_mix, w_in=v_w_in, g_q=v_g_q, g_k=v_g_k, sgu_ln_g=v_sgu_ln_g, sgu_ln_b=v_sgu_ln_b,
               w_spatial=v_w_spatial, b_spatial=v_b_spatial, w_oa=v_w_oa, w_ob=v_w_ob, w_out=v_w_out, g_ff=v_g_ff,
               w_ff1=v_w_ff1, w_ff2=v_w_ff2)

    xs = x[0]
    target = loss_target[0]
    s, d = xs.shape
    depth = g_mix.shape[0]
    sbw = g_q.shape[1] * g_q.shape[2]
    sgw = sgu_ln_g.shape[1]
    d_ff =w_ff1.shape[2] * N_DEV
    n_in = w_in.shape[2] * N_DEV
    gate_off = 3 * sbw + 2 * sgw

    wb = {n: w[n].astype(BF16) for n in BIG}
    full = {n: [None] * depth for n in BIG}

    def arrived(names, l, gathered):
        for n, g in zip(names, gathered):
            full[n][l] = _whole(g, n in COL_SHARDED)

    first = ("w_in", "w_oa", "w_ob", "w_out")
    arrived(first, 0, [_all_gather("gather_" + n, wb[n][0]) for n in first])

    def gather(names, l):
        return [_Exchange(wb[n][l], True) for n in names] if l < depth else []

    res = {}

    def update(n, l, parts):
        res[n] = _adam("adam_" + n, parts, w[n], mom[n], var[n], l, res.get(n))

    g_mix3, g_ff3 = g_mix.reshape(depth, 1, d), g_ff.reshape(depth, 1, d)
    gq3, gk3 = g_q.reshape(depth, 1, sbw), g_k.reshape(depth, 1, sbw)
    ln_g3, ln_b3 = sgu_ln_g.reshape(depth, 1, sgw), sgu_ln_b.reshape(depth, 1, sgw)
    b_col = jnp.broadcast_to(b_spatial[..., None], b_spatial.shape + (SGU_LEN,))

    def tile_out(n, dt):
        return lambda tm, tn: [(SDS((s, n), dt), (tm, tn), lambda i, j, k: (i, j))]

    def tile_in(a):
        return lambda tm, tn: [(a, (tm, tn), lambda i, j, k: (i, j))]

    def gate_tile(proj, off, tm, tn):
        return (proj, (tm, tn), lambda i, j, k: (i, off // tn + j))

    saved = []
    cur = xs
    for l in range(depth):
        h = _rms_fwd("rms_mix", cur, g_mix3, l)
        proj, *got = _mm_fwd("proj_in", h, full["w_in"][l], tile_out(n_in, F32), jobs=gather(["w_ff1"], l))
        arrived(["w_ff1"], l, got)
        qn, kn, vb = _qkv_prep(proj, gq3, gk3, l, sbw)
        o, *got = _attn_fwd(qn, kn, vb, jobs=gather(["w_ff2"], l))
        arrived(["w_ff2"], l, got)
        sg = _sgu_fwd(proj, ln_g3, ln_b3, w_spatial, b_col, l, sbw, sgw)
        ya, yb, merged = _merge_fwd(o, sg, full["w_oa"][l], full["w_ob"][l], proj, d, gate_off)
        x1, *got = _mm_fwd("proj_out", merged, full["w_out"][l], tile_out(d, F32), lambda acc, res: (res + acc,),
                           tile_in(cur), jobs=gather(["w_oa", "w_ob", "w_out"], l + 1))
        arrived(["w_oa", "w_ob", "w_out"], l + 1, got)
        h2 = _rms_fwd("rms_ff", x1, g_ff3, l)
        a1, r = _mm_fwd("ff_up", h2, full["w_ff1"][l],
                        lambda tm, tn: tile_out(d_ff, F32)(tm, tn) + tile_out(d_ff, BF16)(tm, tn),
                        lambda acc: (acc, jnp.square(jnp.maximum(acc, 0.0))))
        x2, *got = _mm_fwd("ff_down", r, full["w_ff2"][l], tile_out(d, F32), lambda acc, res: (res + acc,),
                           tile_in(x1), jobs=gather(["w_in"], l + 1))
        arrived(["w_in"], l + 1, got)
        saved.append(dict(x=cur, h=h, proj=proj, qn=qn, kn=kn, vb=vb, o=o, sg=sg, ya=ya, yb=yb, merged=merged,
                          x1=x1, h2=h2, a1=a1, r=r))
        cur = x2

    dx, dxb, sq = _loss_head(cur, target)
    loss = lax.psum(0.5 * jnp.sum(sq) / d, ("x", "y", "c"))

    gs = {n: [None] * depth for n in SMALL}
    for l in reversed(range(depth)):
        sv = saved[l]
        gw_ff2 = _mm_bwd_w("ff_down_dw", sv["r"], dxb, False)
        da1 = _mm_bwd_x("ff_down_dx", dxb, full["w_ff2"][l], tile_out(d_ff, BF16),
                        lambda acc, a1: (acc * (2.0 * jnp.maximum(a1, 0.0)),), tile_in(sv["a1"]))[0]
        gw_ff1 = _mm_bwd_w("ff_up_dw", sv["h2"], da1, True)
        dh2, parts = _mm_bwd_x("ff_up_dx", da1, full["w_ff1"][l], tile_out(d, F32), jobs=[_Exchange(gw_ff2, False)])
        update("w_ff2", l, parts)
        dx1, dx1b, dg = _rms_bwd("rms_ff_bwd", sv["x1"], g_ff3, l, dh2, dx)
        gs["g_ff"][l] = dg.reshape(d)

        gw_out = _mm_bwd_w("proj_out_dw", sv["merged"], dx1b, False)
        dga, dgb, dya, dyb = _mm_bwd_x(
            "proj_out_dx", dx1b, full["w_out"][l], lambda tm, tn: tile_out(d, BF16)(tm, tn) * 4, _merge_bwd_epilogue,
            lambda tm, tn: [(a, (tm, tn), lambda i, j, k: (i, j)) for a in (sv["ya"], sv["yb"])]
            + [gate_tile(sv["proj"], off, tm, tn) for off in (gate_off, gate_off + d)], tn=512)
        gw_oa = _mm_bwd_w("oa_dw", sv["o"], dya, True)
        gw_ob = _mm_bwd_w("ob_dw", sv["sg"], dyb, True)
        do = _mm_bwd_x("oa_dx", dya, full["w_oa"][l], tile_out(sbw, BF16))[0]
        dsg = _mm_bwd_x("ob_dx", dyb, full["w_ob"][l], tile_out(sgw, F32))[0]

        names = ("w_ff1", "w_out", "w_oa", "w_ob")
        dqn, dkn, dv, *parts = _attn_bwd(sv["qn"], sv["kn"], sv["vb"], do,
                                         jobs=[_Exchange(g, False) for g in (gw_ff1, gw_out, gw_oa, gw_ob)])
        for n, p in zip(names, parts):
            update(n, l, p)
        dq, dk, dvb, dgq, dgk = _qk_bwd(sv["proj"], gq3, gk3, l, sbw, dqn, dkn, dv)
        gs["g_q"][l], gs["g_k"][l] = dgq.reshape(g_q.shape[1:]), dgk.reshape(g_k.shape[1:])
        du, dvs, dlng, dlnb, dwsp, dbsp = _sgu_bwd(sv["proj"], ln_g3, ln_b3, w_spatial, b_col, l, sbw, sgw, dsg)
        gs["sgu_ln_g"][l], gs["sgu_ln_b"][l] = dlng.reshape(sgw), dlnb.reshape(sgw)
        gs["w_spatial"][l], gs["b_spatial"][l] = dwsp, dbsp[:, :, 0]

        dproj = jnp.concatenate([dq, dk, dvb, du, dvs, dga, dgb], axis=1)
        gw = _mm_bwd_w("proj_in_dw", sv["h"], dproj, True)
        dh, parts = _mm_bwd_x("proj_in_dx", dproj, full["w_in"][l], tile_out(d, F32), jobs=[_Exchange(gw, False)])
        update("w_in", l, parts)
        dx, dxb, dg = _rms_bwd("rms_mix_bwd", sv["x"], g_mix3, l, dh, dx1)
        gs["g_mix"][l] = dg.reshape(d)

    small_parts = _all_gather("gather_small", _pack([jnp.stack(gs[n]) for n in SMALL]))
    packed = _adam_small(small_parts, _pack([w[n] for n in SMALL]), _pack([mom[n] for n in SMALL]),
                         _pack([var[n] for n in SMALL]))
    unpacked = [_unpack(p, [w[n] for n in SMALL]) for p in packed]
    for i, n in enumerate(SMALL):
        res[n] = tuple(u[i] for u in unpacked)

    outs = [loss, dx[None]]
    for which in range(4):
        outs += [res[n][which] for n in ORDER]
    return tuple(outs)
```

```python
import functools

import jax
import jax.numpy as jnp
from jax import lax
from jax.experimental import pallas as pl
from jax.experimental.pallas import tpu as pltpu

F32 = jnp.float32
BF16 = jnp.bfloat16
SDS = jax.ShapeDtypeStruct
MESH = pl.DeviceIdType.MESH

N_DEV = 8
HEAD_DIM = 128
GROUP_DIM = 128
SGU_LEN = 128
SGU_CAUSAL = 64
NORM_EPS = 1e-6
ATT_BLOCK = 128
ATT_DEAD = -110.0
ATT_CHAINS = 4

ADAM_LR, ADAM_B1, ADAM_B2, ADAM_EPS, ADAM_WD, ADAM_STEP = 0.001, 0.9, 0.999, 1e-08, 0.01, 10

ROW_TILE = 512
MM_TM, MM_TN, MM_TK = 1024, 1536, 2304


def _tile(n, target, align=128):
    best = None
    for t in range(align, min(n, target) + 1, align):
        if n % t == 0:
            best = t
    return n if best is None else best


class _Exchange:
    def __init__(self, src, gather):
        self.src, self.gather = src, gather
        self.out = SDS(((N_DEV,) + src.shape) if gather else src.shape, src.dtype)
        self.sems = [pltpu.SemaphoreType.DMA((N_DEV - 1,)), pltpu.SemaphoreType.DMA((N_DEV - 1,)),
                     pltpu.SemaphoreType.DMA(())]

    def plan(self, src_ref, out_ref, send_sems, recv_sems, local_sem):
        x, y, c = _position()
        me = 4 * x + 2 * y + c
        pick = (lambda p: src_ref) if self.gather else (lambda p: src_ref.at[p])
        local = pltpu.make_async_copy(pick(me), out_ref.at[me], local_sem)
        sends, arrivals = [], []
        for k in range(1, N_DEV):
            px = 1 - x if k & 4 else x
            py = 1 - y if k & 2 else y
            pc = 1 - c if k & 1 else c
            peer = 4 * px + 2 * py + pc
            for dst, keep in ((out_ref.at[me], sends), (out_ref.at[peer], arrivals)):
                keep.append(pltpu.make_async_remote_copy(
                    src_ref=pick(peer), dst_ref=dst, send_sem=send_sems.at[k - 1], recv_sem=recv_sems.at[k - 1],
                    device_id=(px, py, pc), device_id_type=MESH))
        return local, sends, arrivals


def _call(name, grid, ins, outs, body, scratch=(), semantics=None, jobs=(), aliases=None):
    n_in, n_out, n_scr, n_job = len(ins), len(outs), len(scratch), len(jobs)
    any_spec = pl.BlockSpec(memory_space=pl.ANY)

    def wrapped(*refs):
        own_in, job_in = refs[:n_in], refs[n_in:n_in + n_job]
        p = n_in + n_job
        own_out, job_out = refs[p:p + n_out], refs[p + n_out:p + n_out + n_job]
        p += n_out + n_job
        own_scr, job_sems = refs[p:p + n_scr], refs[p + n_scr:]
        if not jobs:
            return body(*own_in, *own_out, *own_scr)
        ids = [pl.program_id(a) for a in range(len(grid))]
        first = functools.reduce(jnp.logical_and, [i == 0 for i in ids])
        last = functools.reduce(jnp.logical_and, [i == g - 1 for i, g in zip(ids, grid)])
        plans = [jb.plan(job_in[n], job_out[n], *job_sems[3 * n:3 * n + 3]) for n, jb in enumerate(jobs)]

        @pl.when(first)
        def _():
            for local, sends, _ in plans:
                local.start()
                for cp in sends:
                    cp.start()

        body(*own_in, *own_out, *own_scr)

        @pl.when(last)
        def _():
            for local, sends, arrivals in plans:
                for cp in arrivals:
                    cp.wait_recv()
                for cp in sends:
                    cp.wait_send()
                local.wait()

    spec = lambda blk, im: any_spec if blk is None else pl.BlockSpec(blk, im)
    res = pl.pallas_call(
        wrapped, name=name, grid=grid,
        in_specs=[spec(blk, im) for _, blk, im in ins] + [any_spec] * n_job,
        out_specs=[spec(blk, im) for _, blk, im in outs] + [any_spec] * n_job,
        out_shape=[s for s, _, _ in outs] + [jb.out for jb in jobs],
        scratch_shapes=list(scratch) + [sem for jb in jobs for sem in jb.sems],
        input_output_aliases=aliases or {},
        compiler_params=pltpu.CompilerParams(
            dimension_semantics=("arbitrary",) * len(grid) if jobs or not semantics else semantics),
    )(*[a for a, _, _ in ins], *[jb.src for jb in jobs])
    return res


def _dot(a, b, dims=((1,), (0,))):
    return lax.dot_general(a, b, (dims, ((), ())), preferred_element_type=F32)


NN = ((1,), (0,))
NT = ((1,), (1,))
TN = ((0,), (0,))


def _matmul(name, grid, a, b, dims, outs, epilogue=None, extras=(), jobs=()):
    nk = grid[2]
    n_ex, n_out = len(extras), len(outs)
    acc_shape = tuple(d for d in outs[0][1] if d is not None)

    def body(*refs):
        a_ref, b_ref = refs[0], refs[1]
        ex_refs = refs[2:2 + n_ex]
        out_refs = refs[2 + n_ex:2 + n_ex + n_out]
        def product():
            return _dot(a_ref[...].astype(BF16), b_ref[...].astype(BF16), dims)

        def finish(acc):
            vals = (acc,) if epilogue is None else epilogue(acc, *[r[...] for r in ex_refs])
            for r, v in zip(out_refs, vals):
                r[...] = v.astype(r.dtype)

        if nk == 1:
            finish(product())
        else:
            acc_ref = refs[-1]
            k = pl.program_id(2)

            @pl.when(k == 0)
            def _():
                acc_ref[...] = jnp.zeros_like(acc_ref)

            @pl.when(k < nk - 1)
            def _():
                acc_ref[...] += product()

            @pl.when(k == nk - 1)
            def _():
                finish(acc_ref[...] + product())

    return _call(name, grid, [a, b, *extras], list(outs), body,
                 scratch=[pltpu.VMEM(acc_shape, F32)] if nk > 1 else [],
                 semantics=("parallel", "parallel", "arbitrary"), jobs=jobs)


def _whole(gathered, col_sharded):
    if col_sharded:
        return jnp.transpose(gathered, (1, 0, 2)).reshape(gathered.shape[1], -1)
    return gathered.reshape(-1, gathered.shape[2])


def _n_tile(n, target):
    return _tile(n, MM_TN if n % MM_TN == 0 else target)


def _mm_fwd(name, a, w, outs_fn, epilogue=None, extras_fn=None, tn=1024, jobs=()):
    m, k_dim = a.shape
    n_out = w.shape[1]
    tm, tn, tk = _tile(m, MM_TM, 8), _n_tile(n_out, tn), _tile(k_dim, MM_TK)
    grid = (m // tm, n_out // tn, k_dim // tk)
    extras = extras_fn(tm, tn) if extras_fn else ()
    return _matmul(name, grid, (a, (tm, tk), lambda i, j, k: (i, k)), (w, (tk, tn), lambda i, j, k: (k, j)), NN,
                   outs_fn(tm, tn), epilogue, extras, jobs)


def _mm_bwd_x(name, a, w, outs_fn, epilogue=None, extras_fn=None, tn=1024, jobs=()):
    m, k_dim = a.shape
    n_out = w.shape[0]
    tm, tn, tk = _tile(m, MM_TM, 8), _tile(n_out, tn), _tile(k_dim, MM_TK)
    grid = (m // tm, n_out // tn, k_dim // tk)
    extras = extras_fn(tm, tn) if extras_fn else ()
    return _matmul(name, grid, (a, (tm, tk), lambda i, j, k: (i, k)), (w, (tn, tk), lambda i, j, k: (j, k)), NT,
                   outs_fn(tm, tn), epilogue, extras, jobs)


def _mm_bwd_w(name, a, b, col_sharded, jobs=()):
    m, ka = a.shape
    n = b.shape[1]
    ts = _tile(m, 2 * MM_TM, 8)
    shard = (n if col_sharded else ka) // N_DEV
    a_spec = lambda ti: (a, (ts, ti), lambda i, j, k: (k, i))
    b_spec = lambda tj: (b, (ts, tj), lambda i, j, k: (k, j))
    if col_sharded and shard % 1024:
        ti, tj = _tile(ka, 1024), _tile(n, 1024)
        out = (SDS((ka, n), BF16), (ti, tj), lambda i, j, k: (i, j))
        whole, *got = _matmul(name, (ka // ti, n // tj, m // ts), a_spec(ti), b_spec(tj), TN, [out], jobs=jobs)
        return jnp.transpose(whole.reshape(ka, N_DEV, shard), (1, 0, 2)), got
    if col_sharded:
        ti, tj = _tile(ka, 1024), _tile(shard, 1024)
        nps = shard // tj
        out = (SDS((N_DEV, ka, shard), BF16), (None, ti, tj), lambda i, j, k: (j // nps, i, j % nps))
    else:
        ti, tj = _tile(shard, 1024), _tile(n, 1024)
        ips = shard // ti
        out = (SDS((N_DEV, shard, n), BF16), (None, ti, tj), lambda i, j, k: (i // ips, i % ips, j))
    blocks, *got = _matmul(name, (ka // ti, n // tj, m // ts), a_spec(ti), b_spec(tj), TN, [out], jobs=jobs)
    return blocks, got


def _rms_fwd(name, x, g3, l):
    s, d = x.shape
    ts = _tile(s, ROW_TILE, 8)

    def body(x_ref, g_ref, h_ref):
        xv = x_ref[...]
        r = lax.rsqrt(jnp.mean(xv * xv, axis=-1, keepdims=True) + NORM_EPS)
        h_ref[...] = (xv * r * g_ref[...]).astype(BF16)

    return _call(name, (s // ts,), [(x, (ts, d), lambda i: (i, 0)), (g3, (None, 1, d), lambda i: (l, 0, 0))],
                 [(SDS((s, d), BF16), (ts, d), lambda i: (i, 0))], body, semantics=("parallel",))[0]


def _rms_bwd(name, x, g3, l, dh, dres):
    s, d = x.shape
    ts = _tile(s, ROW_TILE, 8)

    def body(x_ref, g_ref, dh_ref, dres_ref, dx_ref, dxb_ref, dg_ref):
        xv = x_ref[...]
        r = lax.rsqrt(jnp.mean(xv * xv, axis=-1, keepdims=True) + NORM_EPS)
        xhat = xv * r
        dhv = dh_ref[...]
        gy = dhv * g_ref[...]
        m = jnp.mean(gy * xhat, axis=-1, keepdims=True)
        dx = dres_ref[...] + r * (gy - xhat * m)
        dx_ref[...] = dx
        dxb_ref[...] = dx.astype(BF16)
        part = jnp.sum(dhv * xhat, axis=0, keepdims=True)

        @pl.when(pl.program_id(0) == 0)
        def _():
            dg_ref[...] = part

        @pl.when(pl.program_id(0) > 0)
        def _():
            dg_ref[...] += part

    row = lambda i: (i, 0)
    return _call(name, (s // ts,),
                 [(x, (ts, d), row), (g3, (None, 1, d), lambda i: (l, 0, 0)), (dh, (ts, d), row), (dres, (ts, d), row)],
                 [(SDS((s, d), F32), (ts, d), row), (SDS((s, d), BF16), (ts, d), row),
                  (SDS((1, d), F32), (1, d), lambda i: (0, 0))], body)


def _loss_head(y, target):
    s, d = y.shape
    ts = _tile(s, ROW_TILE, 8)

    def body(y_ref, t_ref, dy_ref, dyb_ref, sq_ref):
        diff = y_ref[...] - t_ref[...]
        dy = diff / d
        dy_ref[...] = dy
        dyb_ref[...] = dy.astype(BF16)
        part = jnp.sum(diff * diff, axis=0, keepdims=True)

        @pl.when(pl.program_id(0) == 0)
        def _():
            sq_ref[...] = part

        @pl.when(pl.program_id(0) > 0)
        def _():
            sq_ref[...] += part

    row = lambda i: (i, 0)
    return _call("loss_head", (s // ts,), [(y, (ts, d), row), (target, (ts, d), row)],
                 [(SDS((s, d), F32), (ts, d), row), (SDS((s, d), BF16), (ts, d), row),
                  (SDS((1, d), F32), (1, d), lambda i: (0, 0))], body)


def _qkv_prep(proj, gq3, gk3, l, sbw):
    s = proj.shape[0]
    ts = _tile(s, ROW_TILE, 8)
    heads = sbw // HEAD_DIM

    def body(q_ref, k_ref, v_ref, gq_ref, gk_ref, qn_ref, kn_ref, vb_ref):
        for h in range(heads):
            sl = slice(h * HEAD_DIM, (h + 1) * HEAD_DIM)
            for src, g_ref, dst in ((q_ref, gq_ref, qn_ref), (k_ref, gk_ref, kn_ref)):
                t = src[:, sl]
                r = lax.rsqrt(jnp.mean(t * t, axis=-1, keepdims=True) + NORM_EPS)
                dst[:, sl] = (t * r * g_ref[:, sl]).astype(BF16)
        vb_ref[...] = v_ref[...].astype(BF16)

    gspec = lambda g: (g, (None, 1, sbw), lambda i: (l, 0, 0))
    col = lambda c: (proj, (ts, sbw), lambda i: (i, c))
    out = (SDS((s, sbw), BF16), (ts, sbw), lambda i: (i, 0))
    return _call("qkv_prep", (s // ts,), [col(0), col(1), col(2), gspec(gq3), gspec(gk3)], [out, out, out], body,
                 semantics=("parallel",))


def _qk_bwd(proj, gq3, gk3, l, sbw, dqn, dkn, dv):
    s = proj.shape[0]
    ts = _tile(s, ROW_TILE, 8)
    heads = sbw // HEAD_DIM

    def body(q_ref, k_ref, gq_ref, gk_ref, dqn_ref, dkn_ref, dv_ref, dq_ref, dk_ref, dvb_ref, dgq_ref, dgk_ref):
        first = pl.program_id(0) == 0
        for src, g_ref, dn_ref, dst, dg_ref in ((q_ref, gq_ref, dqn_ref, dq_ref, dgq_ref),
                                                (k_ref, gk_ref, dkn_ref, dk_ref, dgk_ref)):
            for h in range(heads):
                sl = slice(h * HEAD_DIM, (h + 1) * HEAD_DIM)
                t = src[:, sl]
                r = lax.rsqrt(jnp.mean(t * t, axis=-1, keepdims=True) + NORM_EPS)
                xhat = t * r
                dn = dn_ref[:, sl]
                gy = dn * g_ref[:, sl]
                m = jnp.mean(gy * xhat, axis=-1, keepdims=True)
                dst[:, sl] = (r * (gy - xhat * m)).astype(BF16)
                part = jnp.sum(dn * xhat, axis=0, keepdims=True)

                @pl.when(first)
                def _():
                    dg_ref[:, sl] = part

                @pl.when(jnp.logical_not(first))
                def _():
                    dg_ref[:, sl] += part
        dvb_ref[...] = dv_ref[...].astype(BF16)

    gspec = lambda g: (g, (None, 1, sbw), lambda i: (l, 0, 0))
    col = lambda c: (proj, (ts, sbw), lambda i: (i, c))
    row = lambda a: (a, (ts, sbw), lambda i: (i, 0))
    outb = (SDS((s, sbw), BF16), (ts, sbw), lambda i: (i, 0))
    outg = (SDS((1, sbw), F32), (1, sbw), lambda i: (0, 0))
    return _call("qk_bwd", (s // ts,), [col(0), col(1), gspec(gq3), gspec(gk3), row(dqn), row(dkn), row(dv)],
                 [outb, outb, outb, outg, outg], body)


def _softplus(z):
    return jnp.maximum(z, 0.0) + jnp.log1p(jnp.exp(-jnp.abs(z)))


def _tri_dot(x, tri):
    hi = x.astype(BF16)
    lo = (x - hi.astype(F32)).astype(BF16)
    return _dot(hi, tri) + _dot(lo, tri)


def _att_common(tb):
    rows = lax.broadcasted_iota(jnp.int32, (tb, tb), 0)
    cols = lax.broadcasted_iota(jnp.int32, (tb, tb), 1)
    suffix = (rows >= cols).astype(BF16)
    prefix = (rows <= cols).astype(BF16)
    return rows, cols, suffix, prefix, jnp.ones((tb, tb), BF16)


def _key_norm_max(k_ref, kmax_ref):
    kf = k_ref[...].astype(F32)
    n2 = jnp.sum(kf * kf, axis=1, keepdims=True)
    kmax_ref[...] = jnp.broadcast_to(jnp.max(n2, axis=0, keepdims=True), kmax_ref.shape)


def _att_reach(q, kmax_ref, scale):
    qf = q.astype(F32)
    qn2 = jnp.sum(qf * qf, axis=1, keepdims=True)
    return scale * jnp.sqrt(qn2 * kmax_ref[0:1, 0:1]) * 1.001 + 1e-3


def _att_chains(s, tb):
    return ATT_CHAINS if s % (ATT_CHAINS * tb) == 0 else 1


def _att_block(diag, j, tb, rows, cols):
    kb = diag - j
    off = pl.multiple_of(jnp.maximum(kb, 0) * tb, tb)
    limit = jnp.where(kb >= 0, diag * tb, -(1 << 30))
    return off, (cols + kb * tb) < (rows + limit)


def _attn_fwd(qn, kn, vb, jobs=()):
    s, sbw = qn.shape
    heads = sbw // HEAD_DIM
    tb = _tile(s, ATT_BLOCK, 8)
    nch = _att_chains(s, tb)
    tq = nch * tb
    scale = HEAD_DIM ** -0.5

    def body(q_ref, k_ref, v_ref, o_ref, kmax_ref, reach_ref, run_ref, acc_ref):
        qi = pl.program_id(1)

        @pl.when(qi == 0)
        def _():
            _key_norm_max(k_ref, kmax_ref)

        rows, cols, suffix, _, ones = _att_common(tb)
        tri = jnp.concatenate([suffix, ones], axis=1)
        diag = [qi * nch + c for c in range(nch)]
        for c in range(nch):
            q = q_ref[c * tb:(c + 1) * tb, :]
            reach_ref[c] = jnp.broadcast_to(_att_reach(q, kmax_ref, scale), (tb, tb))
        run_ref[...] = jnp.zeros_like(run_ref)
        acc_ref[...] = jnp.zeros_like(acc_ref)

        def cond(carry):
            return jnp.logical_and(carry[0] <= diag[-1], carry[1])

        def step(carry):
            j, alive, chains = carry[0], None, range(nch)
            offs, pasts = zip(*[_att_block(diag[c], j, tb, rows, cols) for c in chains])
            runs = [run_ref[c] for c in chains]
            zs = [_dot(q_ref[c * tb:(c + 1) * tb, :], k_ref[pl.ds(offs[c], tb), :], NT) * scale for c in chains]
            keeps = [jnp.where(pasts[c], -_softplus(zs[c]), 0.0) for c in chains]
            sums = [_tri_dot(keeps[c], tri) for c in chains]
            probs = [jnp.where(pasts[c], jnp.exp(zs[c] + sums[c][:, :tb] + runs[c]), 0.0).astype(BF16) for c in chains]
            outs = [_dot(probs[c], v_ref[pl.ds(offs[c], tb), :]) for c in chains]
            for c in chains:
                acc_ref[c] += outs[c]
                run = runs[c] + sums[c][:, tb:]
                run_ref[c] = run
                m = jnp.max(run + reach_ref[c])
                alive = m if alive is None else jnp.maximum(alive, m)
            return j + 1, alive > ATT_DEAD

        lax.while_loop(cond, step, (jnp.int32(0), jnp.bool_(True)))
        for c in range(nch):
            o_ref[c * tb:(c + 1) * tb, :] = acc_ref[c].astype(o_ref.dtype)

    qspec = lambda a: (a, (tq, HEAD_DIM), lambda h, i: (i, h))
    full = lambda a: (a, (s, HEAD_DIM), lambda h, i: (0, h))
    state = pltpu.VMEM((nch, tb, tb), F32)
    return _call("attn_fwd", (heads, s // tq), [qspec(qn), full(kn), full(vb)],
                 [(SDS((s, sbw), BF16), (tq, HEAD_DIM), lambda h, i: (i, h))], body,
                 scratch=[pltpu.VMEM((8, 128), F32), state, state, pltpu.VMEM((nch, tb, HEAD_DIM), F32)],
                 semantics=("parallel", "arbitrary"), jobs=jobs)


def _attn_bwd(qn, kn, vb, do, jobs=()):
    s, sbw = qn.shape
    heads = sbw // HEAD_DIM
    tb = _tile(s, ATT_BLOCK, 8)
    nch = _att_chains(s, tb)
    tq = nch * tb
    scale = HEAD_DIM ** -0.5

    def body(q_ref, k_ref, v_ref, do_ref, dq_ref, dk_ref, dv_ref, kmax_ref, reach_ref, run_ref, g_ref):
        qi = pl.program_id(1)

        @pl.when(qi == 0)
        def _():
            _key_norm_max(k_ref, kmax_ref)
            dk_ref[...] = jnp.zeros_like(dk_ref)
            dv_ref[...] = jnp.zeros_like(dv_ref)

        rows, cols, suffix, prefix, ones = _att_common(tb)
        tri_back = jnp.concatenate([suffix, ones], axis=1)
        tri_fwd = jnp.concatenate([prefix, ones], axis=1)
        diag = [qi * nch + c for c in range(nch)]
        rows_of = lambda ref, c: ref[c * tb:(c + 1) * tb, :]
        for c in range(nch):
            reach_ref[c] = jnp.broadcast_to(_att_reach(rows_of(q_ref, c), kmax_ref, scale), (tb, tb))
        run_ref[...] = jnp.zeros_like(run_ref)
        dq_ref[...] = jnp.zeros_like(dq_ref)

        def cond(carry):
            return jnp.logical_and(carry[0] <= diag[-1], carry[1])

        chains = range(nch)

        def sweep_back(carry):
            j, alive = carry[0], None
            offs, pasts = zip(*[_att_block(diag[c], j, tb, rows, cols) for c in chains])
            runs = [run_ref[c] for c in chains]
            zs = [_dot(rows_of(q_ref, c), k_ref[pl.ds(offs[c], tb), :], NT) * scale for c in chains]
            das = [_dot(rows_of(do_ref, c), v_ref[pl.ds(offs[c], tb), :], NT) for c in chains]
            keeps = [jnp.where(pasts[c], -_softplus(zs[c]), 0.0) for c in chains]
            sums = [_tri_dot(keeps[c], tri_back) for c in chains]
            probs = [jnp.where(pasts[c], jnp.exp(zs[c] + sums[c][:, :tb] + runs[c]), 0.0) for c in chains]
            dvs = [_dot(probs[c].astype(BF16), rows_of(do_ref, c), TN) for c in chains]
            for c in chains:
                g_ref[c, j] = probs[c] * das[c]
                dv_ref[pl.ds(offs[c], tb), :] += dvs[c]
                run = runs[c] + sums[c][:, tb:]
                run_ref[c] = run
                m = jnp.max(run + reach_ref[c])
                alive = m if alive is None else jnp.maximum(alive, m)
            return j + 1, alive > ATT_DEAD

        nsteps = lax.while_loop(cond, sweep_back, (jnp.int32(0), jnp.bool_(True)))[0]
        run_ref[...] = jnp.zeros_like(run_ref)

        def sweep_fwd(i, carry):
            j = nsteps - 1 - i
            offs, pasts = zip(*[_att_block(diag[c], j, tb, rows, cols) for c in chains])
            runs = [run_ref[c] for c in chains]
            ks = [k_ref[pl.ds(offs[c], tb), :] for c in chains]
            sgs = [jax.nn.sigmoid(_dot(rows_of(q_ref, c), ks[c], NT) * scale) for c in chains]
            gs = [g_ref[c, j] for c in chains]
            sums = [_tri_dot(gs[c], tri_fwd) for c in chains]
            dzs = [(jnp.where(pasts[c], gs[c] - sgs[c] * (runs[c] + sums[c][:, :tb]), 0.0) * scale).astype(BF16)
                   for c in chains]
            dqs = [_dot(dzs[c], ks[c]) for c in chains]
            dks = [_dot(dzs[c], rows_of(q_ref, c), TN) for c in chains]
            for c in chains:
                dq_ref[c * tb:(c + 1) * tb, :] += dqs[c]
                dk_ref[pl.ds(offs[c], tb), :] += dks[c]
                run_ref[c] = runs[c] + sums[c][:, tb:]
            return carry

        lax.fori_loop(0, nsteps, sweep_fwd, 0)

    qspec = lambda a: (a, (tq, HEAD_DIM), lambda h, i: (i, h))
    full = lambda a: (a, (s, HEAD_DIM), lambda h, i: (0, h))
    outq = (SDS((s, sbw), F32), (tq, HEAD_DIM), lambda h, i: (i, h))
    outf = (SDS((s, sbw), F32), (s, HEAD_DIM), lambda h, i: (0, h))
    state = pltpu.VMEM((nch, tb, tb), F32)
    return _call("attn_bwd", (heads, s // tq), [qspec(qn), full(kn), full(vb), qspec(do)], [outq, outf, outf], body,
                 scratch=[pltpu.VMEM((8, 128), F32), state, state, pltpu.VMEM((nch, s // tb, tb, tb), F32)],
                 semantics=("parallel", "arbitrary"), jobs=jobs)


_SQRT_HALF = 0.7071067811865476
_INV_SQRT_2PI = 0.3989422804014327


def _gelu(x):
    return 0.5 * x * (1.0 + lax.erf(x * _SQRT_HALF))


def _gelu_grad(x):
    return 0.5 * (1.0 + lax.erf(x * _SQRT_HALF)) + x * jnp.exp(-0.5 * x * x) * _INV_SQRT_2PI


def _sgu_mask():
    i = lax.broadcasted_iota(jnp.int32, (SGU_LEN, SGU_LEN), 0) // SGU_CAUSAL
    j = lax.broadcasted_iota(jnp.int32, (SGU_LEN, SGU_LEN), 1) // SGU_CAUSAL
    return j <= i


def _sgu_norm(vg, lng, lnb):
    mu = jnp.mean(vg, axis=-1, keepdims=True)
    xc = vg - mu
    rstd = lax.rsqrt(jnp.mean(xc * xc, axis=-1, keepdims=True) + NORM_EPS)
    xhat = xc * rstd
    return xhat, rstd, xhat * lng + lnb


def _sgu_specs(proj, ln_g3, ln_b3, w_sp, b_col, l, sbw, sgw, rb):
    groups = sgw // GROUP_DIM
    ucol, vcol = 3 * sbw // sgw, 3 * sbw // sgw + 1
    vec = lambda a: (a, (None, 1, sgw), lambda i: (l, 0, 0))
    mat = lambda a: (a, (None, groups, SGU_LEN, SGU_LEN), lambda i: (l, 0, 0, 0))
    return [(proj, (rb, sgw), lambda i: (i, ucol)), (proj, (rb, sgw), lambda i: (i, vcol)),
            vec(ln_g3), vec(ln_b3), mat(w_sp), mat(b_col)]


def _sgu_fwd(proj, ln_g3, ln_b3, w_sp, b_col, l, sbw, sgw):
    s = proj.shape[0]
    rb = _tile(s, 2 * SGU_LEN, SGU_LEN)
    groups = sgw // GROUP_DIM

    def body(u_ref, v_ref, lng_ref, lnb_ref, w_ref, b_ref, s_ref):
        mask = _sgu_mask()
        _, _, vln = _sgu_norm(_gelu(v_ref[...]), lng_ref[...], lnb_ref[...])
        vb = vln.astype(BF16)
        for g in range(groups):
            cs = slice(g * GROUP_DIM, (g + 1) * GROUP_DIM)
            wm = jnp.where(mask, w_ref[g], 0.0).astype(BF16)
            for c in range(rb // SGU_LEN):
                rs = slice(c * SGU_LEN, (c + 1) * SGU_LEN)
                mixed = _dot(wm, vb[rs, cs]) + b_ref[g]
                s_ref[rs, cs] = (_gelu(u_ref[rs, cs]) * mixed).astype(BF16)

    return _call("sgu_fwd", (s // rb,), _sgu_specs(proj, ln_g3, ln_b3, w_sp, b_col, l, sbw, sgw, rb),
                 [(SDS((s, sgw), BF16), (rb, sgw), lambda i: (i, 0))], body, semantics=("parallel",))[0]


def _sgu_bwd(proj, ln_g3, ln_b3, w_sp, b_col, l, sbw, sgw, ds):
    s = proj.shape[0]
    rb = _tile(s, 2 * SGU_LEN, SGU_LEN)
    groups = sgw // GROUP_DIM
    nsteps = s // rb

    def body(u_ref, v_ref, lng_ref, lnb_ref, w_ref, b_ref, ds_ref,
             du_ref, dv_ref, dlng_ref, dlnb_ref, dw_ref, db_ref, dvln_ref, dw_acc, db_acc):
        step = pl.program_id(0)

        @pl.when(step == 0)
        def _():
            dw_acc[...] = jnp.zeros_like(dw_acc)
            db_acc[...] = jnp.zeros_like(db_acc)

        mask = _sgu_mask()
        vp = v_ref[...]
        lng = lng_ref[...]
        xhat, rstd, vln = _sgu_norm(_gelu(vp), lng, lnb_ref[...])
        vb = vln.astype(BF16)
        for g in range(groups):
            cs = slice(g * GROUP_DIM, (g + 1) * GROUP_DIM)
            wm = jnp.where(mask, w_ref[g], 0.0).astype(BF16)
            for c in range(rb // SGU_LEN):
                rs = slice(c * SGU_LEN, (c + 1) * SGU_LEN)
                vbp = vb[rs, cs]
                mixed = _dot(wm, vbp) + b_ref[g]
                up = u_ref[rs, cs]
                dsp = ds_ref[rs, cs]
                du_ref[rs, cs] = (dsp * mixed * _gelu_grad(up)).astype(BF16)
                dmixed = dsp * _gelu(up)
                dmb = dmixed.astype(BF16)
                dvln_ref[rs, cs] = _dot(wm, dmb, TN)
                dw_acc[g] += _dot(dmb, vbp, NT)
                db_acc[g] += dmixed
        dvln = dvln_ref[...]
        part_g = jnp.sum(dvln * xhat, axis=0, keepdims=True)
        part_b = jnp.sum(dvln, axis=0, keepdims=True)

        @pl.when(step == 0)
        def _():
            dlng_ref[...] = part_g
            dlnb_ref[...] = part_b

        @pl.when(step > 0)
        def _():
            dlng_ref[...] += part_g
            dlnb_ref[...] += part_b

        dxhat = dvln * lng
        m1 = jnp.mean(dxhat, axis=-1, keepdims=True)
        m2 = jnp.mean(dxhat * xhat, axis=-1, keepdims=True)
        dv_ref[...] = (rstd * (dxhat - m1 - xhat * m2) * _gelu_grad(vp)).astype(BF16)

        @pl.when(step == nsteps - 1)
        def _():
            for g in range(groups):
                dw_ref[g] = jnp.where(mask, dw_acc[g], 0.0)
                db_ref[g] = jnp.broadcast_to(jnp.sum(db_acc[g], axis=-1, keepdims=True), (SGU_LEN, SGU_LEN))

    row = lambda i: (i, 0)
    outb = (SDS((s, sgw), BF16), (rb, sgw), row)
    outv = (SDS((1, sgw), F32), (1, sgw), lambda i: (0, 0))
    outm = (SDS((groups, SGU_LEN, SGU_LEN), F32), (groups, SGU_LEN, SGU_LEN), lambda i: (0, 0, 0))
    acc = pltpu.VMEM((groups, SGU_LEN, SGU_LEN), F32)
    return _call("sgu_bwd", (nsteps,),
                 _sgu_specs(proj, ln_g3, ln_b3, w_sp, b_col, l, sbw, sgw, rb) + [(ds, (rb, sgw), row)],
                 [outb, outb, outv, outv, outm, outm], body, scratch=[pltpu.VMEM((rb, sgw), F32), acc, acc])


def _merge_fwd(o, sg, w_oa, w_ob, proj, d, gate_off, jobs=()):
    s, kw = o.shape
    tm, tn = _tile(s, MM_TM, 8), _tile(d, 512)

    def body(o_ref, s_ref, wa_ref, wb_ref, ga_ref, gb_ref, ya_ref, yb_ref, mg_ref):
        ya = _dot(o_ref[...], wa_ref[...])
        yb = _dot(s_ref[...], wb_ref[...])
        ya_ref[...] = ya
        yb_ref[...] = yb
        mg_ref[...] = (jax.nn.sigmoid(ga_ref[...]) * ya + jax.nn.sigmoid(gb_ref[...]) * yb).astype(BF16)

    act = lambda a: (a, (tm, kw), lambda i, j: (i, 0))
    wsp = lambda w: (w, (kw, tn), lambda i, j: (0, j))
    gate = lambda off: (proj, (tm, tn), lambda i, j: (i, off // tn + j))
    tile = lambda dt: (SDS((s, d), dt), (tm, tn), lambda i, j: (i, j))
    return _call("merge_fwd", (s // tm, d // tn), [act(o), act(sg), wsp(w_oa), wsp(w_ob), gate(gate_off), gate(gate_off + d)],
                 [tile(F32), tile(F32), tile(BF16)], body, semantics=("parallel", "parallel"), jobs=jobs)


def _merge_bwd_epilogue(dm, ya, yb, ga, gb):
    sa, sb = jax.nn.sigmoid(ga), jax.nn.sigmoid(gb)
    return dm * ya * sa * (1.0 - sa), dm * yb * sb * (1.0 - sb), dm * sa, dm * sb


def _position():
    x, y, c = lax.axis_index("x"), lax.axis_index("y"), lax.axis_index("c")
    return x, y, c


def _all_gather(name, shard):
    def body(x_ref, out_ref, send_sems, recv_sems, local_sem):
        x, y, c = _position()
        me, sibling = (x, y, c), (x, y, 1 - c)
        chips = [(1 - x, y), (x, 1 - y), (1 - x, 1 - y)]

        def slot(px, py, pc):
            return out_ref.at[4 * px + 2 * py + pc]

        def copy(k, block, to, src=None):
            return pltpu.make_async_remote_copy(
                src_ref=slot(*block) if src is None else src, dst_ref=slot(*block),
                send_sem=send_sems.at[k], recv_sem=recv_sems.at[k], device_id=to, device_id_type=MESH)

        mine = pltpu.make_async_copy(x_ref, slot(*me), local_sem)
        mine.start()
        first = [copy(0, me, sibling, src=x_ref)]
        first += [copy(1 + j, me, (*chip, c), src=x_ref) for j, chip in enumerate(chips)]
        for cp in first:
            cp.start()
        passed = [copy(4 + j, (*chip, c), sibling) for j, chip in enumerate(chips)]
        for j, chip in enumerate(chips):
            copy(1 + j, (*chip, c), me).wait_recv()
            passed[j].start()
        copy(0, sibling, me).wait_recv()
        for j, chip in enumerate(chips):
            copy(4 + j, (*chip, 1 - c), me).wait_recv()
        for cp in first + passed:
            cp.wait_send()
        mine.wait()

    return pl.pallas_call(
        body, name=name, out_shape=SDS((N_DEV,) + shard.shape, shard.dtype),
        in_specs=[pl.BlockSpec(memory_space=pl.ANY)], out_specs=pl.BlockSpec(memory_space=pl.ANY),
        scratch_shapes=[pltpu.SemaphoreType.DMA((7,)), pltpu.SemaphoreType.DMA((7,)), pltpu.SemaphoreType.DMA(())],
    )(shard)


def _adam_math(g, w, m, v):
    m = ADAM_B1 * m + (1.0 - ADAM_B1) * g
    v = ADAM_B2 * v + (1.0 - ADAM_B2) * (g * g)
    m_hat = m / (1.0 - ADAM_B1 ** ADAM_STEP)
    v_hat = v / (1.0 - ADAM_B2 ** ADAM_STEP)
    delta = -ADAM_LR * (m_hat / (jnp.sqrt(v_hat) + ADAM_EPS) + ADAM_WD * w)
    return delta, m, v


def _adam(name, parts, w, m, v, l, prev, row0=0):
    _, r, c = parts.shape
    tr, tc = _tile(r, 256, 8), _tile(c, MM_TN)
    rb = row0 // tr

    def body(p_ref, w_ref, m_ref, v_ref, *rest):
        g_ref, d_ref, mo_ref, vo_ref = rest[-4:]
        g = p_ref[0].astype(F32)
        for q in range(1, N_DEV):
            g = g + p_ref[q].astype(F32)
        delta, mn, vn = _adam_math(g, w_ref[...], m_ref[...], v_ref[...])
        g_ref[...] = g
        d_ref[...] = delta
        mo_ref[...] = mn
        vo_ref[...] = vn

    cur = lambda a: (a, (None, tr, tc), lambda i, j: (l, rb + i, j))
    out = (SDS(w.shape, F32), (None, tr, tc), lambda i, j: (l, rb + i, j))
    ins = [(parts, (N_DEV, tr, tc), lambda i, j: (0, i, j)), cur(w), cur(m), cur(v)]
    aliases = None
    if prev is not None:
        ins += [(p, None, None) for p in prev]
        aliases = {4 + n: n for n in range(4)}
    return _call(name, (r // tr, c // tc), ins, [out, out, out, out], body, semantics=("parallel", "parallel"),
                 aliases=aliases)


def _adam_small(parts, w, m, v):
    r = w.shape[0]
    tr = _tile(r, 1096, 8)

    def body(p_ref, w_ref, m_ref, v_ref, g_ref, d_ref, mo_ref, vo_ref):
        g = p_ref[0]
        for q in range(1, N_DEV):
            g = g + p_ref[q]
        delta, mn, vn = _adam_math(g, w_ref[...], m_ref[...], v_ref[...])
        g_ref[...] = g
        d_ref[...] = delta
        mo_ref[...] = mn
        vo_ref[...] = vn

    cur = lambda a: (a, (tr, 128), lambda i: (i, 0))
    out = (SDS(w.shape, F32), (tr, 128), lambda i: (i, 0))
    return _call("adam_small", (r // tr,), [(parts, (N_DEV, tr, 128), lambda i: (0, i, 0)), cur(w), cur(m), cur(v)],
                 [out, out, out, out], body, semantics=("parallel",))


SMALL = ("g_mix", "g_q", "g_k", "sgu_ln_g", "sgu_ln_b", "w_spatial", "b_spatial", "g_ff")
BIG = ("w_in", "w_oa", "w_ob", "w_out", "w_ff1", "w_ff2")
COL_SHARDED = ("w_in", "w_oa", "w_ob", "w_ff1")
ORDER = ("g_mix", "w_in", "g_q", "g_k", "sgu_ln_g", "sgu_ln_b", "w_spatial", "b_spatial", "w_oa", "w_ob", "w_out",
         "g_ff", "w_ff1", "w_ff2")


def _pack(arrs):
    return jnp.concatenate([a.reshape(-1) for a in arrs]).reshape(-1, 128)


def _unpack(packed, like):
    flat, out, off = packed.reshape(-1), [], 0
    for a in like:
        out.append(flat[off:off + a.size].reshape(a.shape))
        off += a.size
    return out


def kernel(x, g_mix, w_in, g_q, g_k, sgu_ln_g, sgu_ln_b, w_spatial, b_spatial, w_oa, w_ob, w_out, g_ff, w_ff1, w_ff2, loss_target, m_g_mix, m_w_in, m_g_q, m_g_k, m_sgu_ln_g, m_sgu_ln_b, m_w_spatial, m_b_spatial, m_w_oa, m_w_ob, m_w_out, m_g_ff, m_w_ff1, m_w_ff2, v_g_mix, v_w_in, v_g_q, v_g_k, v_sgu_ln_g, v_sgu_ln_b, v_w_spatial, v_b_spatial, v_w_oa, v_w_ob, v_w_out, v_g_ff, v_w_ff1, v_w_ff2):
    w = dict(g_mix=g_mix, w_in=w_in, g_q=g_q, g_k=g_k, sgu_ln_g=sgu_ln_g, sgu_ln_b=sgu_ln_b, w_spatial=w_spatial,
             b_spatial=b_spatial, w_oa=w_oa, w_ob=w_ob, w_out=w_out, g_ff=g_ff, w_ff1=w_ff1, w_ff2=w_ff2)
    mom = dict(g_mix=m_g_mix, w_in=m_w_in, g_q=m_g_q, g_k=m_g_k, sgu_ln_g=m_sgu_ln_g, sgu_ln_b=m_sgu_ln_b,
               w_spatial=m_w_spatial, b_spatial=m_b_spatial, w_oa=m_w_oa, w_ob=m_w_ob, w_out=m_w_out, g_ff=m_g_ff,
               w_ff1=m_w_ff1, w_ff2=m_w_ff2)
    var = dict(g_mix=v_g_mix, w_in=v_w_in, g_q=v_g_q, g_k=v_g_k, sgu_ln_g=v_sgu_ln_g, sgu_ln_b=v_sgu_ln_b,
               w_spatial=v_w_spatial, b_spatial=v_b_spatial, w_oa=v_w_oa, w_ob=v_w_ob, w_out=v_w_out, g_ff=v_g_ff,
               w_ff1=v_w_ff1, w_ff2=v_w_ff2)

    xs = x[0]
    target = loss_target[0]
    s, d = xs.shape
    depth = g_mix.shape[0]
    sbw = g_q.shape[1] * g_q.shape[2]
    sgw = sgu_ln_g.shape[1]
    d_ff =w_ff1.shape[2] * N_DEV
    n_in = w_in.shape[2] * N_DEV
    gate_off = 3 * sbw + 2 * sgw

    wb = {n: w[n].astype(BF16) for n in BIG}
    full = {n: [None] * depth for n in BIG}

    def arrived(names, l, gathered):
        for n, g in zip(names, gathered):
            full[n][l] = _whole(g, n in COL_SHARDED)

    first = ("w_in", "w_oa", "w_ob", "w_out")
    arrived(first, 0, [_all_gather("gather_" + n, wb[n][0]) for n in first])

    def gather(names, l):
        return [_Exchange(wb[n][l], True) for n in names] if l < depth else []

    def gather_half(n, l, which):
        r = wb[n].shape[1] // 2
        return [_Exchange(wb[n][l, which * r:(which + 1) * r], True)] if l < depth else []

    def halves(blocks):
        r = blocks.shape[1] // 2
        return blocks[:, :r], blocks[:, r:], r

    res = {}

    def update(n, l, parts, row0=0):
        res[n] = _adam("adam_" + n, parts, w[n], mom[n], var[n], l, res.get(n), row0)

    g_mix3, g_ff3 = g_mix.reshape(depth, 1, d), g_ff.reshape(depth, 1, d)
    gq3, gk3 = g_q.reshape(depth, 1, sbw), g_k.reshape(depth, 1, sbw)
    ln_g3, ln_b3 = sgu_ln_g.reshape(depth, 1, sgw), sgu_ln_b.reshape(depth, 1, sgw)
    b_col = jnp.broadcast_to(b_spatial[..., None], b_spatial.shape + (SGU_LEN,))

    def tile_out(n, dt):
        return lambda tm, tn: [(SDS((s, n), dt), (tm, tn), lambda i, j, k: (i, j))]

    def tile_in(a):
        return lambda tm, tn: [(a, (tm, tn), lambda i, j, k: (i, j))]

    def gate_tile(proj, off, tm, tn):
        return (proj, (tm, tn), lambda i, j, k: (i, off // tn + j))

    saved = []
    cur = xs
    for l in range(depth):
        h = _rms_fwd("rms_mix", cur, g_mix3, l)
        proj, *got = _mm_fwd("proj_in", h, full["w_in"][l], tile_out(n_in, F32), jobs=gather(["w_ff1"], l))
        arrived(["w_ff1"], l, got)
        qn, kn, vb = _qkv_prep(proj, gq3, gk3, l, sbw)
        o, *got = _attn_fwd(qn, kn, vb, jobs=gather(["w_ff2"], l))
        arrived(["w_ff2"], l, got)
        sg = _sgu_fwd(proj, ln_g3, ln_b3, w_spatial, b_col, l, sbw, sgw)
        ya, yb, merged, *got = _merge_fwd(o, sg, full["w_oa"][l], full["w_ob"][l], proj, d, gate_off,
                                          jobs=gather(["w_oa", "w_ob"], l + 1))
        arrived(["w_oa", "w_ob"], l + 1, got)
        x1, *got = _mm_fwd("proj_out", merged, full["w_out"][l], tile_out(d, F32), lambda acc, res: (res + acc,),
                           tile_in(cur), jobs=gather(["w_out"], l + 1))
        arrived(["w_out"], l + 1, got)
        h2 = _rms_fwd("rms_ff", x1, g_ff3, l)
        a1, r, *upper = _mm_fwd("ff_up", h2, full["w_ff1"][l],
                                lambda tm, tn: tile_out(d_ff, F32)(tm, tn) + tile_out(d_ff, BF16)(tm, tn),
                                lambda acc: (acc, jnp.square(jnp.maximum(acc, 0.0))), jobs=gather_half("w_in", l + 1, 0))
        x2, *lower = _mm_fwd("ff_down", r, full["w_ff2"][l], tile_out(d, F32), lambda acc, res: (res + acc,),
                             tile_in(x1), jobs=gather_half("w_in", l + 1, 1))
        if upper:
            full["w_in"][l + 1] = jnp.concatenate([_whole(upper[0], True), _whole(lower[0], True)], axis=0)
        saved.append(dict(x=cur, h=h, proj=proj, qn=qn, kn=kn, vb=vb, o=o, sg=sg, ya=ya, yb=yb, merged=merged,
                          x1=x1, h2=h2, a1=a1, r=r))
        cur = x2

    dx, dxb, sq = _loss_head(cur, target)
    loss = lax.psum(0.5 * jnp.sum(sq) / d, ("x", "y", "c"))

    gs = {n: [None] * depth for n in SMALL}
    late = None
    for l in reversed(range(depth)):
        sv = saved[l]
        gw_ff2, got = _mm_bwd_w("ff_down_dw", sv["r"], dxb, False, jobs=[_Exchange(late[1], False)] if late else [])
        if late:
            update("w_in", late[0], got[0], late[2])
        upper, lower, row0 = halves(gw_ff2)
        da1, parts = _mm_bwd_x("ff_down_dx", dxb, full["w_ff2"][l], tile_out(d_ff, BF16),
                               lambda acc, a1: (acc * (2.0 * jnp.maximum(a1, 0.0)),), tile_in(sv["a1"]),
                               jobs=[_Exchange(upper, False)])
        update("w_ff2", l, parts)
        gw_ff1, got = _mm_bwd_w("ff_up_dw", sv["h2"], da1, True, jobs=[_Exchange(lower, False)])
        update("w_ff2", l, got[0], row0)
        upper, lower_ff1, row0_ff1 = halves(gw_ff1)
        dh2, parts = _mm_bwd_x("ff_up_dx", da1, full["w_ff1"][l], tile_out(d, F32), jobs=[_Exchange(upper, False)])
        update("w_ff1", l, parts)
        dx1, dx1b, dg = _rms_bwd("rms_ff_bwd", sv["x1"], g_ff3, l, dh2, dx)
        gs["g_ff"][l] = dg.reshape(d)

        gw_out = _mm_bwd_w("proj_out_dw", sv["merged"], dx1b, False)[0]
        dga, dgb, dya, dyb = _mm_bwd_x(
            "proj_out_dx", dx1b, full["w_out"][l], lambda tm, tn: tile_out(d, BF16)(tm, tn) * 4, _merge_bwd_epilogue,
            lambda tm, tn: [(a, (tm, tn), lambda i, j, k: (i, j)) for a in (sv["ya"], sv["yb"])]
            + [gate_tile(sv["proj"], off, tm, tn) for off in (gate_off, gate_off + d)], tn=512)
        gw_oa = _mm_bwd_w("oa_dw", sv["o"], dya, True)[0]
        gw_ob = _mm_bwd_w("ob_dw", sv["sg"], dyb, True)[0]
        do = _mm_bwd_x("oa_dx", dya, full["w_oa"][l], tile_out(sbw, BF16))[0]
        dsg = _mm_bwd_x("ob_dx", dyb, full["w_ob"][l], tile_out(sgw, F32))[0]

        dqn, dkn, dv, *parts = _attn_bwd(sv["qn"], sv["kn"], sv["vb"], do,
                                         jobs=[_Exchange(g, False) for g in (lower_ff1, gw_out, gw_oa, gw_ob)])
        update("w_ff1", l, parts[0], row0_ff1)
        for n, p in zip(("w_out", "w_oa", "w_ob"), parts[1:]):
            update(n, l, p)
        dq, dk, dvb, dgq, dgk = _qk_bwd(sv["proj"], gq3, gk3, l, sbw, dqn, dkn, dv)
        gs["g_q"][l], gs["g_k"][l] = dgq.reshape(g_q.shape[1:]), dgk.reshape(g_k.shape[1:])
        du, dvs, dlng, dlnb, dwsp, dbsp = _sgu_bwd(sv["proj"], ln_g3, ln_b3, w_spatial, b_col, l, sbw, sgw, dsg)
        gs["sgu_ln_g"][l], gs["sgu_ln_b"][l] = dlng.reshape(sgw), dlnb.reshape(sgw)
        gs["w_spatial"][l], gs["b_spatial"][l] = dwsp, dbsp[:, :, 0]

        dproj = jnp.concatenate([dq, dk, dvb, du, dvs, dga, dgb], axis=1)
        upper, lower, row0 = halves(_mm_bwd_w("proj_in_dw", sv["h"], dproj, True)[0])
        late = (l, lower, row0) if l > 0 else None
        dh, *parts = _mm_bwd_x("proj_in_dx", dproj, full["w_in"][l], tile_out(d, F32),
                               jobs=[_Exchange(g, False) for g in ((upper,) if late else (upper, lower))])
        update("w_in", l, parts[0])
        if not late:
            update("w_in", l, parts[1], row0)
        dx, dxb, dg = _rms_bwd("rms_mix_bwd", sv["x"], g_mix3, l, dh, dx1)
        gs["g_mix"][l] = dg.reshape(d)

    small_parts = _all_gather("gather_small", _pack([jnp.stack(gs[n]) for n in SMALL]))
    packed = _adam_small(small_parts, _pack([w[n] for n in SMALL]), _pack([mom[n] for n in SMALL]),
                         _pack([var[n] for n in SMALL]))
    unpacked = [_unpack(p, [w[n] for n in SMALL]) for p in packed]
    for i, n in enumerate(SMALL):
        res[n] = tuple(u[i] for u in unpacked)

    outs = [loss, dx[None]]
    for which in range(4):
        outs += [res[n][which] for n in ORDER]
    return tuple(outs)
```

```python
import functools

import jax
import jax.numpy as jnp
from jax import lax
from jax.experimental import pallas as pl
from jax.experimental.pallas import tpu as pltpu

F32 = jnp.float32
BF16 = jnp.bfloat16
SDS = jax.ShapeDtypeStruct
MESH = pl.DeviceIdType.MESH

N_DEV = 8
HEAD_DIM = 128
GROUP_DIM = 128
SGU_LEN = 128
SGU_CAUSAL = 64
NORM_EPS = 1e-6
ATT_BLOCK = 128
ATT_DEAD = -110.0
ATT_CHAINS = 4

ADAM_LR, ADAM_B1, ADAM_B2, ADAM_EPS, ADAM_WD, ADAM_STEP = 0.001, 0.9, 0.999, 1e-08, 0.01, 10

ROW_TILE = 512
MM_TM, MM_TN, MM_TK = 1024, 1536, 2304


def _tile(n, target, align=128):
    best = None
    for t in range(align, min(n, target) + 1, align):
        if n % t == 0:
            best = t
    return n if best is None else best


class _Exchange:
    def __init__(self, src, kind, rows=None, into=None):
        self.src, self.kind, self.rows, self.into = src, kind, rows, into
        n = rows[1] if rows else src.shape[-2]
        width = src.shape[-1]
        shape = {"rows": (N_DEV,) + src.shape, "cols": (src.shape[0], N_DEV * width),
                 "blocks": (N_DEV, n, width), "slabs": (N_DEV, n, width // N_DEV)}[kind]
        self.out = SDS(shape, src.dtype)
        self.sems = [pltpu.SemaphoreType.DMA((N_DEV - 1,)), pltpu.SemaphoreType.DMA((N_DEV - 1,)),
                     pltpu.SemaphoreType.DMA(())]

    def plan(self, src_ref, out_ref, send_sems, recv_sems, local_sem):
        x, y, c = _position()
        me = 4 * x + 2 * y + c
        part = pl.ds(*self.rows) if self.rows else slice(None)
        width = self.out.shape[-1] // N_DEV if self.kind == "cols" else self.out.shape[-1]
        slab = lambda q: pl.ds(q * width, width)
        if self.kind == "rows":
            src_of, dst_of = (lambda p: src_ref.at[part]), (lambda q: out_ref.at[q, part])
        elif self.kind == "cols":
            src_of, dst_of = (lambda p: src_ref.at[part]), (lambda q: out_ref.at[part, slab(q)])
        elif self.kind == "blocks":
            src_of, dst_of = (lambda p: src_ref.at[p, part]), (lambda q: out_ref.at[q])
        else:
            src_of, dst_of = (lambda p: src_ref.at[part, slab(p)]), (lambda q: out_ref.at[q])
        local = pltpu.make_async_copy(src_of(me), dst_of(me), local_sem)
        sends, arrivals = [], []
        for k in range(1, N_DEV):
            px = 1 - x if k & 4 else x
            py = 1 - y if k & 2 else y
            pc = 1 - c if k & 1 else c
            peer = 4 * px + 2 * py + pc
            for dst, keep in ((dst_of(me), sends), (dst_of(peer), arrivals)):
                keep.append(pltpu.make_async_remote_copy(
                    src_ref=src_of(peer), dst_ref=dst, send_sem=send_sems.at[k - 1], recv_sem=recv_sems.at[k - 1],
                    device_id=(px, py, pc), device_id_type=MESH))
        return local, sends, arrivals


def _call(name, grid, ins, outs, body, scratch=(), semantics=None, jobs=(), aliases=None):
    n_in, n_out, n_scr, n_job = len(ins), len(outs), len(scratch), len(jobs)
    any_spec = pl.BlockSpec(memory_space=pl.ANY)
    aliases = dict(aliases or {})
    completed = [(n, jb.into) for n, jb in enumerate(jobs) if jb.into is not None]
    for pos, (n, _) in enumerate(completed):
        aliases[n_in + n_job + pos] = n_out + n

    def wrapped(*refs):
        own_in, job_in = refs[:n_in], refs[n_in:n_in + n_job]
        p = n_in + n_job + len(completed)
        own_out, job_out = refs[p:p + n_out], refs[p + n_out:p + n_out + n_job]
        p += n_out + n_job
        own_scr, job_sems = refs[p:p + n_scr], refs[p + n_scr:]
        if not jobs:
            return body(*own_in, *own_out, *own_scr)
        ids = [pl.program_id(a) for a in range(len(grid))]
        first = functools.reduce(jnp.logical_and, [i == 0 for i in ids])
        last = functools.reduce(jnp.logical_and, [i == g - 1 for i, g in zip(ids, grid)])
        plans = [jb.plan(job_in[n], job_out[n], *job_sems[3 * n:3 * n + 3]) for n, jb in enumerate(jobs)]

        @pl.when(first)
        def _():
            for local, sends, _ in plans:
                local.start()
                for cp in sends:
                    cp.start()

        body(*own_in, *own_out, *own_scr)

        @pl.when(last)
        def _():
            for local, sends, arrivals in plans:
                for cp in arrivals:
                    cp.wait_recv()
                for cp in sends:
                    cp.wait_send()
                local.wait()

    spec = lambda blk, im: any_spec if blk is None else pl.BlockSpec(blk, im)
    res = pl.pallas_call(
        wrapped, name=name, grid=grid,
        in_specs=[spec(blk, im) for _, blk, im in ins] + [any_spec] * (n_job + len(completed)),
        out_specs=[spec(blk, im) for _, blk, im in outs] + [any_spec] * n_job,
        out_shape=[s for s, _, _ in outs] + [jb.out for jb in jobs],
        scratch_shapes=list(scratch) + [sem for jb in jobs for sem in jb.sems],
        input_output_aliases=aliases,
        compiler_params=pltpu.CompilerParams(
            dimension_semantics=("arbitrary",) * len(grid) if jobs or not semantics else semantics),
    )(*[a for a, _, _ in ins], *[jb.src for jb in jobs], *[into for _, into in completed])
    return res


def _dot(a, b, dims=((1,), (0,))):
    return lax.dot_general(a, b, (dims, ((), ())), preferred_element_type=F32)


NN = ((1,), (0,))
NT = ((1,), (1,))
TN = ((0,), (0,))


def _matmul(name, grid, a, b, dims, outs, epilogue=None, extras=(), jobs=()):
    nk = grid[2]
    n_ex, n_out = len(extras), len(outs)
    acc_shape = tuple(d for d in outs[0][1] if d is not None)

    def body(*refs):
        a_ref, b_ref = refs[0], refs[1]
        ex_refs = refs[2:2 + n_ex]
        out_refs = refs[2 + n_ex:2 + n_ex + n_out]
        def product():
            return _dot(a_ref[...].astype(BF16), b_ref[...].astype(BF16), dims)

        def finish(acc):
            vals = (acc,) if epilogue is None else epilogue(acc, *[r[...] for r in ex_refs])
            for r, v in zip(out_refs, vals):
                r[...] = v.astype(r.dtype)

        if nk == 1:
            finish(product())
        else:
            acc_ref = refs[-1]
            k = pl.program_id(2)

            @pl.when(k == 0)
            def _():
                acc_ref[...] = jnp.zeros_like(acc_ref)

            @pl.when(k < nk - 1)
            def _():
                acc_ref[...] += product()

            @pl.when(k == nk - 1)
            def _():
                finish(acc_ref[...] + product())

    return _call(name, grid, [a, b, *extras], list(outs), body,
                 scratch=[pltpu.VMEM(acc_shape, F32)] if nk > 1 else [],
                 semantics=("parallel", "parallel", "arbitrary"), jobs=jobs)


def _whole(gathered, col_sharded):
    if col_sharded:
        return jnp.transpose(gathered, (1, 0, 2)).reshape(gathered.shape[1], -1)
    return gathered.reshape(-1, gathered.shape[2])


def _n_tile(n, target):
    return _tile(n, MM_TN if n % MM_TN == 0 else target)


def _mm_fwd(name, a, w, outs_fn, epilogue=None, extras_fn=None, tn=1024, jobs=()):
    m, k_dim = a.shape
    n_out = w.shape[1]
    tm, tn, tk = _tile(m, MM_TM, 8), _n_tile(n_out, tn), _tile(k_dim, MM_TK)
    grid = (m // tm, n_out // tn, k_dim // tk)
    extras = extras_fn(tm, tn) if extras_fn else ()
    return _matmul(name, grid, (a, (tm, tk), lambda i, j, k: (i, k)), (w, (tk, tn), lambda i, j, k: (k, j)), NN,
                   outs_fn(tm, tn), epilogue, extras, jobs)


def _mm_bwd_x(name, a, w, outs_fn, epilogue=None, extras_fn=None, tn=1024, jobs=()):
    m, k_dim = a.shape
    n_out = w.shape[0]
    tm, tn, tk = _tile(m, MM_TM, 8), _tile(n_out, tn), _tile(k_dim, MM_TK)
    grid = (m // tm, n_out // tn, k_dim // tk)
    extras = extras_fn(tm, tn) if extras_fn else ()
    return _matmul(name, grid, (a, (tm, tk), lambda i, j, k: (i, k)), (w, (tn, tk), lambda i, j, k: (j, k)), NT,
                   outs_fn(tm, tn), epilogue, extras, jobs)


def _mm_bwd_w(name, a, b, col_sharded, jobs=()):
    m, ka = a.shape
    n = b.shape[1]
    ts = _tile(m, 2 * MM_TM, 8)
    tj = _tile(n, 1024)
    if col_sharded:
        ti = _tile(ka, 1024)
        out = (SDS((ka, n), BF16), (ti, tj), lambda i, j, k: (i, j))
    else:
        shard = ka // N_DEV
        ti = _tile(shard, 1024)
        ips = shard // ti
        out = (SDS((N_DEV, shard, n), BF16), (None, ti, tj), lambda i, j, k: (i // ips, i % ips, j))
    grad, *got = _matmul(name, (ka // ti, n // tj, m // ts), (a, (ts, ti), lambda i, j, k: (k, i)),
                         (b, (ts, tj), lambda i, j, k: (k, j)), TN, [out], jobs=jobs)
    return grad, got


def _rms_fwd(name, x, g3, l):
    s, d = x.shape
    ts = _tile(s, ROW_TILE, 8)

    def body(x_ref, g_ref, h_ref):
        xv = x_ref[...]
        r = lax.rsqrt(jnp.mean(xv * xv, axis=-1, keepdims=True) + NORM_EPS)
        h_ref[...] = (xv * r * g_ref[...]).astype(BF16)

    return _call(name, (s // ts,), [(x, (ts, d), lambda i: (i, 0)), (g3, (None, 1, d), lambda i: (l, 0, 0))],
                 [(SDS((s, d), BF16), (ts, d), lambda i: (i, 0))], body, semantics=("parallel",))[0]


def _rms_bwd(name, x, g3, l, dh, dres):
    s, d = x.shape
    ts = _tile(s, ROW_TILE, 8)

    def body(x_ref, g_ref, dh_ref, dres_ref, dx_ref, dxb_ref, dg_ref):
        xv = x_ref[...]
        r = lax.rsqrt(jnp.mean(xv * xv, axis=-1, keepdims=True) + NORM_EPS)
        xhat = xv * r
        dhv = dh_ref[...]
        gy = dhv * g_ref[...]
        m = jnp.mean(gy * xhat, axis=-1, keepdims=True)
        dx = dres_ref[...] + r * (gy - xhat * m)
        dx_ref[...] = dx
        dxb_ref[...] = dx.astype(BF16)
        part = jnp.sum(dhv * xhat, axis=0, keepdims=True)

        @pl.when(pl.program_id(0) == 0)
        def _():
            dg_ref[...] = part

        @pl.when(pl.program_id(0) > 0)
        def _():
            dg_ref[...] += part

    row = lambda i: (i, 0)
    return _call(name, (s // ts,),
                 [(x, (ts, d), row), (g3, (None, 1, d), lambda i: (l, 0, 0)), (dh, (ts, d), row), (dres, (ts, d), row)],
                 [(SDS((s, d), F32), (ts, d), row), (SDS((s, d), BF16), (ts, d), row),
                  (SDS((1, d), F32), (1, d), lambda i: (0, 0))], body)


def _loss_head(y, target):
    s, d = y.shape
    ts = _tile(s, ROW_TILE, 8)

    def body(y_ref, t_ref, dy_ref, dyb_ref, sq_ref):
        diff = y_ref[...] - t_ref[...]
        dy = diff / d
        dy_ref[...] = dy
        dyb_ref[...] = dy.astype(BF16)
        part = jnp.sum(diff * diff, axis=0, keepdims=True)

        @pl.when(pl.program_id(0) == 0)
        def _():
            sq_ref[...] = part

        @pl.when(pl.program_id(0) > 0)
        def _():
            sq_ref[...] += part

    row = lambda i: (i, 0)
    return _call("loss_head", (s // ts,), [(y, (ts, d), row), (target, (ts, d), row)],
                 [(SDS((s, d), F32), (ts, d), row), (SDS((s, d), BF16), (ts, d), row),
                  (SDS((1, d), F32), (1, d), lambda i: (0, 0))], body)


def _qkv_prep(proj, gq3, gk3, l, sbw):
    s = proj.shape[0]
    ts = _tile(s, ROW_TILE, 8)
    heads = sbw // HEAD_DIM

    def body(q_ref, k_ref, v_ref, gq_ref, gk_ref, qn_ref, kn_ref, vb_ref):
        for h in range(heads):
            sl = slice(h * HEAD_DIM, (h + 1) * HEAD_DIM)
            for src, g_ref, dst in ((q_ref, gq_ref, qn_ref), (k_ref, gk_ref, kn_ref)):
                t = src[:, sl]
                r = lax.rsqrt(jnp.mean(t * t, axis=-1, keepdims=True) + NORM_EPS)
                dst[:, sl] = (t * r * g_ref[:, sl]).astype(BF16)
        vb_ref[...] = v_ref[...].astype(BF16)

    gspec = lambda g: (g, (None, 1, sbw), lambda i: (l, 0, 0))
    col = lambda c: (proj, (ts, sbw), lambda i: (i, c))
    out = (SDS((s, sbw), BF16), (ts, sbw), lambda i: (i, 0))
    return _call("qkv_prep", (s // ts,), [col(0), col(1), col(2), gspec(gq3), gspec(gk3)], [out, out, out], body,
                 semantics=("parallel",))


def _qk_bwd(proj, gq3, gk3, l, sbw, dqn, dkn, dv):
    s = proj.shape[0]
    ts = _tile(s, ROW_TILE, 8)
    heads = sbw // HEAD_DIM

    def body(q_ref, k_ref, gq_ref, gk_ref, dqn_ref, dkn_ref, dv_ref, dq_ref, dk_ref, dvb_ref, dgq_ref, dgk_ref):
        first = pl.program_id(0) == 0
        for src, g_ref, dn_ref, dst, dg_ref in ((q_ref, gq_ref, dqn_ref, dq_ref, dgq_ref),
                                                (k_ref, gk_ref, dkn_ref, dk_ref, dgk_ref)):
            for h in range(heads):
                sl = slice(h * HEAD_DIM, (h + 1) * HEAD_DIM)
                t = src[:, sl]
                r = lax.rsqrt(jnp.mean(t * t, axis=-1, keepdims=True) + NORM_EPS)
                xhat = t * r
                dn = dn_ref[:, sl]
                gy = dn * g_ref[:, sl]
                m = jnp.mean(gy * xhat, axis=-1, keepdims=True)
                dst[:, sl] = (r * (gy - xhat * m)).astype(BF16)
                part = jnp.sum(dn * xhat, axis=0, keepdims=True)

                @pl.when(first)
                def _():
                    dg_ref[:, sl] = part

                @pl.when(jnp.logical_not(first))
                def _():
                    dg_ref[:, sl] += part
        dvb_ref[...] = dv_ref[...].astype(BF16)

    gspec = lambda g: (g, (None, 1, sbw), lambda i: (l, 0, 0))
    col = lambda c: (proj, (ts, sbw), lambda i: (i, c))
    row = lambda a: (a, (ts, sbw), lambda i: (i, 0))
    outb = (SDS((s, sbw), BF16), (ts, sbw), lambda i: (i, 0))
    outg = (SDS((1, sbw), F32), (1, sbw), lambda i: (0, 0))
    return _call("qk_bwd", (s // ts,), [col(0), col(1), gspec(gq3), gspec(gk3), row(dqn), row(dkn), row(dv)],
                 [outb, outb, outb, outg, outg], body)


def _softplus(z):
    return jnp.maximum(z, 0.0) + jnp.log1p(jnp.exp(-jnp.abs(z)))


def _tri_dot(x, tri):
    hi = x.astype(BF16)
    lo = (x - hi.astype(F32)).astype(BF16)
    return _dot(hi, tri) + _dot(lo, tri)


def _att_common(tb):
    rows = lax.broadcasted_iota(jnp.int32, (tb, tb), 0)
    cols = lax.broadcasted_iota(jnp.int32, (tb, tb), 1)
    suffix = (rows >= cols).astype(BF16)
    prefix = (rows <= cols).astype(BF16)
    return rows, cols, suffix, prefix, jnp.ones((tb, tb), BF16)


def _key_norm_max(k_ref, kmax_ref):
    kf = k_ref[...].astype(F32)
    n2 = jnp.sum(kf * kf, axis=1, keepdims=True)
    kmax_ref[...] = jnp.broadcast_to(jnp.max(n2, axis=0, keepdims=True), kmax_ref.shape)


def _att_reach(q, kmax_ref, scale):
    qf = q.astype(F32)
    qn2 = jnp.sum(qf * qf, axis=1, keepdims=True)
    return scale * jnp.sqrt(qn2 * kmax_ref[0:1, 0:1]) * 1.001 + 1e-3


def _att_chains(s, tb):
    return ATT_CHAINS if s % (ATT_CHAINS * tb) == 0 else 1


def _att_block(diag, j, tb, rows, cols):
    kb = diag - j
    off = pl.multiple_of(jnp.maximum(kb, 0) * tb, tb)
    limit = jnp.where(kb >= 0, diag * tb, -(1 << 30))
    return off, (cols + kb * tb) < (rows + limit)


def _attn_fwd(qn, kn, vb, jobs=()):
    s, sbw = qn.shape
    heads = sbw // HEAD_DIM
    tb = _tile(s, ATT_BLOCK, 8)
    nch = _att_chains(s, tb)
    tq = nch * tb
    scale = HEAD_DIM ** -0.5

    def body(q_ref, k_ref, v_ref, o_ref, kmax_ref, reach_ref, run_ref, acc_ref):
        qi = pl.program_id(1)

        @pl.when(qi == 0)
        def _():
            _key_norm_max(k_ref, kmax_ref)

        rows, cols, suffix, _, ones = _att_common(tb)
        tri = jnp.concatenate([suffix, ones], axis=1)
        diag = [qi * nch + c for c in range(nch)]
        for c in range(nch):
            q = q_ref[c * tb:(c + 1) * tb, :]
            reach_ref[c] = jnp.broadcast_to(_att_reach(q, kmax_ref, scale), (tb, tb))
        run_ref[...] = jnp.zeros_like(run_ref)
        acc_ref[...] = jnp.zeros_like(acc_ref)

        def cond(carry):
            return jnp.logical_and(carry[0] <= diag[-1], carry[1])

        def step(carry):
            j, alive, chains = carry[0], None, range(nch)
            offs, pasts = zip(*[_att_block(diag[c], j, tb, rows, cols) for c in chains])
            runs = [run_ref[c] for c in chains]
            zs = [_dot(q_ref[c * tb:(c + 1) * tb, :], k_ref[pl.ds(offs[c], tb), :], NT) * scale for c in chains]
            keeps = [jnp.where(pasts[c], -_softplus(zs[c]), 0.0) for c in chains]
            sums = [_tri_dot(keeps[c], tri) for c in chains]
            probs = [jnp.where(pasts[c], jnp.exp(zs[c] + sums[c][:, :tb] + runs[c]), 0.0).astype(BF16) for c in chains]
            outs = [_dot(probs[c], v_ref[pl.ds(offs[c], tb), :]) for c in chains]
            for c in chains:
                acc_ref[c] += outs[c]
                run = runs[c] + sums[c][:, tb:]
                run_ref[c] = run
                m = jnp.max(run + reach_ref[c])
                alive = m if alive is None else jnp.maximum(alive, m)
            return j + 1, alive > ATT_DEAD

        lax.while_loop(cond, step, (jnp.int32(0), jnp.bool_(True)))
        for c in range(nch):
            o_ref[c * tb:(c + 1) * tb, :] = acc_ref[c].astype(o_ref.dtype)

    qspec = lambda a: (a, (tq, HEAD_DIM), lambda h, i: (i, h))
    full = lambda a: (a, (s, HEAD_DIM), lambda h, i: (0, h))
    state = pltpu.VMEM((nch, tb, tb), F32)
    return _call("attn_fwd", (heads, s // tq), [qspec(qn), full(kn), full(vb)],
                 [(SDS((s, sbw), BF16), (tq, HEAD_DIM), lambda h, i: (i, h))], body,
                 scratch=[pltpu.VMEM((8, 128), F32), state, state, pltpu.VMEM((nch, tb, HEAD_DIM), F32)],
                 semantics=("parallel", "arbitrary"), jobs=jobs)


def _attn_bwd(qn, kn, vb, do, jobs=()):
    s, sbw = qn.shape
    heads = sbw // HEAD_DIM
    tb = _tile(s, ATT_BLOCK, 8)
    nch = _att_chains(s, tb)
    tq = nch * tb
    scale = HEAD_DIM ** -0.5

    def body(q_ref, k_ref, v_ref, do_ref, dq_ref, dk_ref, dv_ref, kmax_ref, reach_ref, run_ref, g_ref):
        qi = pl.program_id(1)

        @pl.when(qi == 0)
        def _():
            _key_norm_max(k_ref, kmax_ref)
            dk_ref[...] = jnp.zeros_like(dk_ref)
            dv_ref[...] = jnp.zeros_like(dv_ref)

        rows, cols, suffix, prefix, ones = _att_common(tb)
        tri_back = jnp.concatenate([suffix, ones], axis=1)
        tri_fwd = jnp.concatenate([prefix, ones], axis=1)
        diag = [qi * nch + c for c in range(nch)]
        rows_of = lambda ref, c: ref[c * tb:(c + 1) * tb, :]
        for c in range(nch):
            reach_ref[c] = jnp.broadcast_to(_att_reach(rows_of(q_ref, c), kmax_ref, scale), (tb, tb))
        run_ref[...] = jnp.zeros_like(run_ref)
        dq_ref[...] = jnp.zeros_like(dq_ref)

        def cond(carry):
            return jnp.logical_and(carry[0] <= diag[-1], carry[1])

        chains = range(nch)

        def sweep_back(carry):
            j, alive = carry[0], None
            offs, pasts = zip(*[_att_block(diag[c], j, tb, rows, cols) for c in chains])
            runs = [run_ref[c] for c in chains]
            zs = [_dot(rows_of(q_ref, c), k_ref[pl.ds(offs[c], tb), :], NT) * scale for c in chains]
            das = [_dot(rows_of(do_ref, c), v_ref[pl.ds(offs[c], tb), :], NT) for c in chains]
            keeps = [jnp.where(pasts[c], -_softplus(zs[c]), 0.0) for c in chains]
            sums = [_tri_dot(keeps[c], tri_back) for c in chains]
            probs = [jnp.where(pasts[c], jnp.exp(zs[c] + sums[c][:, :tb] + runs[c]), 0.0) for c in chains]
            dvs = [_dot(probs[c].astype(BF16), rows_of(do_ref, c), TN) for c in chains]
            for c in chains:
                g_ref[c, j] = probs[c] * das[c]
                dv_ref[pl.ds(offs[c], tb), :] += dvs[c]
                run = runs[c] + sums[c][:, tb:]
                run_ref[c] = run
                m = jnp.max(run + reach_ref[c])
                alive = m if alive is None else jnp.maximum(alive, m)
            return j + 1, alive > ATT_DEAD

        nsteps = lax.while_loop(cond, sweep_back, (jnp.int32(0), jnp.bool_(True)))[0]
        run_ref[...] = jnp.zeros_like(run_ref)

        def sweep_fwd(i, carry):
            j = nsteps - 1 - i
            offs, pasts = zip(*[_att_block(diag[c], j, tb, rows, cols) for c in chains])
            runs = [run_ref[c] for c in chains]
            ks = [k_ref[pl.ds(offs[c], tb), :] for c in chains]
            sgs = [jax.nn.sigmoid(_dot(rows_of(q_ref, c), ks[c], NT) * scale) for c in chains]
            gs = [g_ref[c, j] for c in chains]
            sums = [_tri_dot(gs[c], tri_fwd) for c in chains]
            dzs = [(jnp.where(pasts[c], gs[c] - sgs[c] * (runs[c] + sums[c][:, :tb]), 0.0) * scale).astype(BF16)
                   for c in chains]
            dqs = [_dot(dzs[c], ks[c]) for c in chains]
            dks = [_dot(dzs[c], rows_of(q_ref, c), TN) for c in chains]
            for c in chains:
                dq_ref[c * tb:(c + 1) * tb, :] += dqs[c]
                dk_ref[pl.ds(offs[c], tb), :] += dks[c]
                run_ref[c] = runs[c] + sums[c][:, tb:]
            return carry

        lax.fori_loop(0, nsteps, sweep_fwd, 0)

    qspec = lambda a: (a, (tq, HEAD_DIM), lambda h, i: (i, h))
    full = lambda a: (a, (s, HEAD_DIM), lambda h, i: (0, h))
    outq = (SDS((s, sbw), F32), (tq, HEAD_DIM), lambda h, i: (i, h))
    outf = (SDS((s, sbw), F32), (s, HEAD_DIM), lambda h, i: (0, h))
    state = pltpu.VMEM((nch, tb, tb), F32)
    return _call("attn_bwd", (heads, s // tq), [qspec(qn), full(kn), full(vb), qspec(do)], [outq, outf, outf], body,
                 scratch=[pltpu.VMEM((8, 128), F32), state, state, pltpu.VMEM((nch, s // tb, tb, tb), F32)],
                 semantics=("parallel", "arbitrary"), jobs=jobs)


_SQRT_HALF = 0.7071067811865476
_INV_SQRT_2PI = 0.3989422804014327


def _gelu(x):
    return 0.5 * x * (1.0 + lax.erf(x * _SQRT_HALF))


def _gelu_grad(x):
    return 0.5 * (1.0 + lax.erf(x * _SQRT_HALF)) + x * jnp.exp(-0.5 * x * x) * _INV_SQRT_2PI


def _sgu_mask():
    i = lax.broadcasted_iota(jnp.int32, (SGU_LEN, SGU_LEN), 0) // SGU_CAUSAL
    j = lax.broadcasted_iota(jnp.int32, (SGU_LEN, SGU_LEN), 1) // SGU_CAUSAL
    return j <= i


def _sgu_norm(vg, lng, lnb):
    mu = jnp.mean(vg, axis=-1, keepdims=True)
    xc = vg - mu
    rstd = lax.rsqrt(jnp.mean(xc * xc, axis=-1, keepdims=True) + NORM_EPS)
    xhat = xc * rstd
    return xhat, rstd, xhat * lng + lnb


def _sgu_specs(proj, ln_g3, ln_b3, w_sp, b_col, l, sbw, sgw, rb):
    groups = sgw // GROUP_DIM
    ucol, vcol = 3 * sbw // sgw, 3 * sbw // sgw + 1
    vec = lambda a: (a, (None, 1, sgw), lambda i: (l, 0, 0))
    mat = lambda a: (a, (None, groups, SGU_LEN, SGU_LEN), lambda i: (l, 0, 0, 0))
    return [(proj, (rb, sgw), lambda i: (i, ucol)), (proj, (rb, sgw), lambda i: (i, vcol)),
            vec(ln_g3), vec(ln_b3), mat(w_sp), mat(b_col)]


def _sgu_fwd(proj, ln_g3, ln_b3, w_sp, b_col, l, sbw, sgw):
    s = proj.shape[0]
    rb = _tile(s, 2 * SGU_LEN, SGU_LEN)
    groups = sgw // GROUP_DIM

    def body(u_ref, v_ref, lng_ref, lnb_ref, w_ref, b_ref, s_ref):
        mask = _sgu_mask()
        _, _, vln = _sgu_norm(_gelu(v_ref[...]), lng_ref[...], lnb_ref[...])
        vb = vln.astype(BF16)
        for g in range(groups):
            cs = slice(g * GROUP_DIM, (g + 1) * GROUP_DIM)
            wm = jnp.where(mask, w_ref[g], 0.0).astype(BF16)
            for c in range(rb // SGU_LEN):
                rs = slice(c * SGU_LEN, (c + 1) * SGU_LEN)
                mixed = _dot(wm, vb[rs, cs]) + b_ref[g]
                s_ref[rs, cs] = (_gelu(u_ref[rs, cs]) * mixed).astype(BF16)

    return _call("sgu_fwd", (s // rb,), _sgu_specs(proj, ln_g3, ln_b3, w_sp, b_col, l, sbw, sgw, rb),
                 [(SDS((s, sgw), BF16), (rb, sgw), lambda i: (i, 0))], body, semantics=("parallel",))[0]


def _sgu_bwd(proj, ln_g3, ln_b3, w_sp, b_col, l, sbw, sgw, ds):
    s = proj.shape[0]
    rb = _tile(s, 2 * SGU_LEN, SGU_LEN)
    groups = sgw // GROUP_DIM
    nsteps = s // rb

    def body(u_ref, v_ref, lng_ref, lnb_ref, w_ref, b_ref, ds_ref,
             du_ref, dv_ref, dlng_ref, dlnb_ref, dw_ref, db_ref, dvln_ref, dw_acc, db_acc):
        step = pl.program_id(0)

        @pl.when(step == 0)
        def _():
            dw_acc[...] = jnp.zeros_like(dw_acc)
            db_acc[...] = jnp.zeros_like(db_acc)

        mask = _sgu_mask()
        vp = v_ref[...]
        lng = lng_ref[...]
        xhat, rstd, vln = _sgu_norm(_gelu(vp), lng, lnb_ref[...])
        vb = vln.astype(BF16)
        for g in range(groups):
            cs = slice(g * GROUP_DIM, (g + 1) * GROUP_DIM)
            wm = jnp.where(mask, w_ref[g], 0.0).astype(BF16)
            for c in range(rb // SGU_LEN):
                rs = slice(c * SGU_LEN, (c + 1) * SGU_LEN)
                vbp = vb[rs, cs]
                mixed = _dot(wm, vbp) + b_ref[g]
                up = u_ref[rs, cs]
                dsp = ds_ref[rs, cs]
                du_ref[rs, cs] = (dsp * mixed * _gelu_grad(up)).astype(BF16)
                dmixed = dsp * _gelu(up)
                dmb = dmixed.astype(BF16)
                dvln_ref[rs, cs] = _dot(wm, dmb, TN)
                dw_acc[g] += _dot(dmb, vbp, NT)
                db_acc[g] += dmixed
        dvln = dvln_ref[...]
        part_g = jnp.sum(dvln * xhat, axis=0, keepdims=True)
        part_b = jnp.sum(dvln, axis=0, keepdims=True)

        @pl.when(step == 0)
        def _():
            dlng_ref[...] = part_g
            dlnb_ref[...] = part_b

        @pl.when(step > 0)
        def _():
            dlng_ref[...] += part_g
            dlnb_ref[...] += part_b

        dxhat = dvln * lng
        m1 = jnp.mean(dxhat, axis=-1, keepdims=True)
        m2 = jnp.mean(dxhat * xhat, axis=-1, keepdims=True)
        dv_ref[...] = (rstd * (dxhat - m1 - xhat * m2) * _gelu_grad(vp)).astype(BF16)

        @pl.when(step == nsteps - 1)
        def _():
            for g in range(groups):
                dw_ref[g] = jnp.where(mask, dw_acc[g], 0.0)
                db_ref[g] = jnp.broadcast_to(jnp.sum(db_acc[g], axis=-1, keepdims=True), (SGU_LEN, SGU_LEN))

    row = lambda i: (i, 0)
    outb = (SDS((s, sgw), BF16), (rb, sgw), row)
    outv = (SDS((1, sgw), F32), (1, sgw), lambda i: (0, 0))
    outm = (SDS((groups, SGU_LEN, SGU_LEN), F32), (groups, SGU_LEN, SGU_LEN), lambda i: (0, 0, 0))
    acc = pltpu.VMEM((groups, SGU_LEN, SGU_LEN), F32)
    return _call("sgu_bwd", (nsteps,),
                 _sgu_specs(proj, ln_g3, ln_b3, w_sp, b_col, l, sbw, sgw, rb) + [(ds, (rb, sgw), row)],
                 [outb, outb, outv, outv, outm, outm], body, scratch=[pltpu.VMEM((rb, sgw), F32), acc, acc])


def _merge_fwd(o, sg, w_oa, w_ob, proj, d, gate_off, jobs=()):
    s, kw = o.shape
    tm, tn = _tile(s, MM_TM, 8), _tile(d, 512)

    def body(o_ref, s_ref, wa_ref, wb_ref, ga_ref, gb_ref, ya_ref, yb_ref, mg_ref):
        ya = _dot(o_ref[...], wa_ref[...])
        yb = _dot(s_ref[...], wb_ref[...])
        ya_ref[...] = ya.astype(ya_ref.dtype)
        yb_ref[...] = yb.astype(yb_ref.dtype)
        mg_ref[...] = (jax.nn.sigmoid(ga_ref[...]) * ya + jax.nn.sigmoid(gb_ref[...]) * yb).astype(BF16)

    act = lambda a: (a, (tm, kw), lambda i, j: (i, 0))
    wsp = lambda w: (w, (kw, tn), lambda i, j: (0, j))
    gate = lambda off: (proj, (tm, tn), lambda i, j: (i, off // tn + j))
    tile = lambda dt: (SDS((s, d), dt), (tm, tn), lambda i, j: (i, j))
    return _call("merge_fwd", (s // tm, d // tn), [act(o), act(sg), wsp(w_oa), wsp(w_ob), gate(gate_off), gate(gate_off + d)],
                 [tile(BF16), tile(BF16), tile(BF16)], body, semantics=("parallel", "parallel"), jobs=jobs)


def _merge_bwd_epilogue(dm, ya, yb, ga, gb):
    sa, sb = jax.nn.sigmoid(ga), jax.nn.sigmoid(gb)
    return dm * ya * sa * (1.0 - sa), dm * yb * sb * (1.0 - sb), dm * sa, dm * sb


def _position():
    x, y, c = lax.axis_index("x"), lax.axis_index("y"), lax.axis_index("c")
    return x, y, c


def _all_gather(name, shard):
    def body(x_ref, out_ref, send_sems, recv_sems, local_sem):
        x, y, c = _position()
        me, sibling = (x, y, c), (x, y, 1 - c)
        chips = [(1 - x, y), (x, 1 - y), (1 - x, 1 - y)]

        def slot(px, py, pc):
            return out_ref.at[4 * px + 2 * py + pc]

        def copy(k, block, to, src=None):
            return pltpu.make_async_remote_copy(
                src_ref=slot(*block) if src is None else src, dst_ref=slot(*block),
                send_sem=send_sems.at[k], recv_sem=recv_sems.at[k], device_id=to, device_id_type=MESH)

        mine = pltpu.make_async_copy(x_ref, slot(*me), local_sem)
        mine.start()
        first = [copy(0, me, sibling, src=x_ref)]
        first += [copy(1 + j, me, (*chip, c), src=x_ref) for j, chip in enumerate(chips)]
        for cp in first:
            cp.start()
        passed = [copy(4 + j, (*chip, c), sibling) for j, chip in enumerate(chips)]
        for j, chip in enumerate(chips):
            copy(1 + j, (*chip, c), me).wait_recv()
            passed[j].start()
        copy(0, sibling, me).wait_recv()
        for j, chip in enumerate(chips):
            copy(4 + j, (*chip, 1 - c), me).wait_recv()
        for cp in first + passed:
            cp.wait_send()
        mine.wait()

    return pl.pallas_call(
        body, name=name, out_shape=SDS((N_DEV,) + shard.shape, shard.dtype),
        in_specs=[pl.BlockSpec(memory_space=pl.ANY)], out_specs=pl.BlockSpec(memory_space=pl.ANY),
        scratch_shapes=[pltpu.SemaphoreType.DMA((7,)), pltpu.SemaphoreType.DMA((7,)), pltpu.SemaphoreType.DMA(())],
    )(shard)


def _adam_math(g, w, m, v):
    m = ADAM_B1 * m + (1.0 - ADAM_B1) * g
    v = ADAM_B2 * v + (1.0 - ADAM_B2) * (g * g)
    m_hat = m / (1.0 - ADAM_B1 ** ADAM_STEP)
    v_hat = v / (1.0 - ADAM_B2 ** ADAM_STEP)
    delta = -ADAM_LR * (m_hat / (jnp.sqrt(v_hat) + ADAM_EPS) + ADAM_WD * w)
    return delta, m, v


def _adam(name, parts, w, m, v, l, prev, row0=0):
    _, r, c = parts.shape
    tr, tc = _tile(r, 256, 8), _tile(c, MM_TN)
    rb = row0 // tr

    def body(p_ref, w_ref, m_ref, v_ref, *rest):
        g_ref, d_ref, mo_ref, vo_ref = rest[-4:]
        g = p_ref[0].astype(F32)
        for q in range(1, N_DEV):
            g = g + p_ref[q].astype(F32)
        delta, mn, vn = _adam_math(g, w_ref[...], m_ref[...], v_ref[...])
        g_ref[...] = g
        d_ref[...] = delta
        mo_ref[...] = mn
        vo_ref[...] = vn

    cur = lambda a: (a, (None, tr, tc), lambda i, j: (l, rb + i, j))
    out = (SDS(w.shape, F32), (None, tr, tc), lambda i, j: (l, rb + i, j))
    ins = [(parts, (N_DEV, tr, tc), lambda i, j: (0, i, j)), cur(w), cur(m), cur(v)]
    aliases = None
    if prev is not None:
        ins += [(p, None, None) for p in prev]
        aliases = {4 + n: n for n in range(4)}
    return _call(name, (r // tr, c // tc), ins, [out, out, out, out], body, semantics=("parallel", "parallel"),
                 aliases=aliases)


def _adam_small(parts, w, m, v):
    r = w.shape[0]
    tr = _tile(r, 1096, 8)

    def body(p_ref, w_ref, m_ref, v_ref, g_ref, d_ref, mo_ref, vo_ref):
        g = p_ref[0]
        for q in range(1, N_DEV):
            g = g + p_ref[q]
        delta, mn, vn = _adam_math(g, w_ref[...], m_ref[...], v_ref[...])
        g_ref[...] = g
        d_ref[...] = delta
        mo_ref[...] = mn
        vo_ref[...] = vn

    cur = lambda a: (a, (tr, 128), lambda i: (i, 0))
    out = (SDS(w.shape, F32), (tr, 128), lambda i: (i, 0))
    return _call("adam_small", (r // tr,), [(parts, (N_DEV, tr, 128), lambda i: (0, i, 0)), cur(w), cur(m), cur(v)],
                 [out, out, out, out], body, semantics=("parallel",))


SMALL = ("g_mix", "g_q", "g_k", "sgu_ln_g", "sgu_ln_b", "w_spatial", "b_spatial", "g_ff")
BIG = ("w_in", "w_oa", "w_ob", "w_out", "w_ff1", "w_ff2")
COL_SHARDED = ("w_in", "w_oa", "w_ob", "w_ff1")
ORDER = ("g_mix", "w_in", "g_q", "g_k", "sgu_ln_g", "sgu_ln_b", "w_spatial", "b_spatial", "w_oa", "w_ob", "w_out",
         "g_ff", "w_ff1", "w_ff2")


def _pack(arrs):
    return jnp.concatenate([a.reshape(-1) for a in arrs]).reshape(-1, 128)


def _unpack(packed, like):
    flat, out, off = packed.reshape(-1), [], 0
    for a in like:
        out.append(flat[off:off + a.size].reshape(a.shape))
        off += a.size
    return out


def kernel(x, g_mix, w_in, g_q, g_k, sgu_ln_g, sgu_ln_b, w_spatial, b_spatial, w_oa, w_ob, w_out, g_ff, w_ff1, w_ff2, loss_target, m_g_mix, m_w_in, m_g_q, m_g_k, m_sgu_ln_g, m_sgu_ln_b, m_w_spatial, m_b_spatial, m_w_oa, m_w_ob, m_w_out, m_g_ff, m_w_ff1, m_w_ff2, v_g_mix, v_w_in, v_g_q, v_g_k, v_sgu_ln_g, v_sgu_ln_b, v_w_spatial, v_b_spatial, v_w_oa, v_w_ob, v_w_out, v_g_ff, v_w_ff1, v_w_ff2):
    w = dict(g_mix=g_mix, w_in=w_in, g_q=g_q, g_k=g_k, sgu_ln_g=sgu_ln_g, sgu_ln_b=sgu_ln_b, w_spatial=w_spatial,
             b_spatial=b_spatial, w_oa=w_oa, w_ob=w_ob, w_out=w_out, g_ff=g_ff, w_ff1=w_ff1, w_ff2=w_ff2)
    mom = dict(g_mix=m_g_mix, w_in=m_w_in, g_q=m_g_q, g_k=m_g_k, sgu_ln_g=m_sgu_ln_g, sgu_ln_b=m_sgu_ln_b,
               w_spatial=m_w_spatial, b_spatial=m_b_spatial, w_oa=m_w_oa, w_ob=m_w_ob, w_out=m_w_out, g_ff=m_g_ff,
               w_ff1=m_w_ff1, w_ff2=m_w_ff2)
    var = dict(g_mix=v_g_mix, w_in=v_w_in, g_q=v_g_q, g_k=v_g_k, sgu_ln_g=v_sgu_ln_g, sgu_ln_b=v_sgu_ln_b,
               w_spatial=v_w_spatial, b_spatial=v_b_spatial, w_oa=v_w_oa, w_ob=v_w_ob, w_out=v_w_out, g_ff=v_g_ff,
               w_ff1=v_w_ff1, w_ff2=v_w_ff2)

    xs = x[0]
    target = loss_target[0]
    s, d = xs.shape
    depth = g_mix.shape[0]
    sbw = g_q.shape[1] * g_q.shape[2]
    sgw = sgu_ln_g.shape[1]
    d_ff =w_ff1.shape[2] * N_DEV
    n_in = w_in.shape[2] * N_DEV
    gate_off = 3 * sbw + 2 * sgw

    wb = {n: w[n].astype(BF16) for n in BIG}
    full = {n: [None] * depth for n in BIG}

    for n in ("w_in", "w_oa", "w_ob", "w_out"):
        full[n][0] = _whole(_all_gather("gather_" + n, wb[n][0]), n in COL_SHARDED)

    def gather(names, l, rows=None, into=None):
        if l >= depth:
            return []
        return [_Exchange(wb[n][l], "cols" if n in COL_SHARDED else "rows", rows, into) for n in names]

    def arrived(names, l, gathered):
        for n, g in zip(names, gathered):
            full[n][l] = g if n in COL_SHARDED else g.reshape(-1, g.shape[2])

    def scatter(grad, col_sharded, rows=None):
        return _Exchange(grad, "slabs" if col_sharded else "blocks", rows)

    res = {}

    def update(n, l, parts, row0=0):
        res[n] = _adam("adam_" + n, parts, w[n], mom[n], var[n], l, res.get(n), row0)

    g_mix3, g_ff3 = g_mix.reshape(depth, 1, d), g_ff.reshape(depth, 1, d)
    gq3, gk3 = g_q.reshape(depth, 1, sbw), g_k.reshape(depth, 1, sbw)
    ln_g3, ln_b3 = sgu_ln_g.reshape(depth, 1, sgw), sgu_ln_b.reshape(depth, 1, sgw)
    b_col = jnp.broadcast_to(b_spatial[..., None], b_spatial.shape + (SGU_LEN,))

    def tile_out(n, dt):
        return lambda tm, tn: [(SDS((s, n), dt), (tm, tn), lambda i, j, k: (i, j))]

    def tile_in(a):
        return lambda tm, tn: [(a, (tm, tn), lambda i, j, k: (i, j))]

    def gate_tile(proj, off, tm, tn):
        return (proj, (tm, tn), lambda i, j, k: (i, off // tn + j))

    saved = []
    cur = xs
    for l in range(depth):
        h = _rms_fwd("rms_mix", cur, g_mix3, l)
        proj, *got = _mm_fwd("proj_in", h, full["w_in"][l], tile_out(n_in, F32), jobs=gather(["w_ff1"], l))
        arrived(["w_ff1"], l, got)
        qn, kn, vb = _qkv_prep(proj, gq3, gk3, l, sbw)
        o, *got = _attn_fwd(qn, kn, vb, jobs=gather(["w_ff2"], l))
        arrived(["w_ff2"], l, got)
        sg = _sgu_fwd(proj, ln_g3, ln_b3, w_spatial, b_col, l, sbw, sgw)
        ya, yb, merged, *got = _merge_fwd(o, sg, full["w_oa"][l], full["w_ob"][l], proj, d, gate_off,
                                          jobs=gather(["w_oa", "w_ob"], l + 1))
        arrived(["w_oa", "w_ob"], l + 1, got)
        x1, *got = _mm_fwd("proj_out", merged, full["w_out"][l], tile_out(d, F32), lambda acc, res: (res + acc,),
                           tile_in(cur), jobs=gather(["w_out"], l + 1))
        arrived(["w_out"], l + 1, got)
        h2 = _rms_fwd("rms_ff", x1, g_ff3, l)
        half = d // 2
        a1, r, *got = _mm_fwd("ff_up", h2, full["w_ff1"][l],
                              lambda tm, tn: tile_out(d_ff, F32)(tm, tn) + tile_out(d_ff, BF16)(tm, tn),
                              lambda acc: (acc, jnp.square(jnp.maximum(acc, 0.0))),
                              jobs=gather(["w_in"], l + 1, (0, half)))
        x2, *got = _mm_fwd("ff_down", r, full["w_ff2"][l], tile_out(d, F32), lambda acc, res: (res + acc,),
                           tile_in(x1), jobs=gather(["w_in"], l + 1, (half, half), got[0] if got else None))
        arrived(["w_in"], l + 1, got)
        saved.append(dict(x=cur, h=h, proj=proj, qn=qn, kn=kn, vb=vb, o=o, sg=sg, ya=ya, yb=yb, merged=merged,
                          x1=x1, h2=h2, a1=a1, r=r))
        cur = x2

    dx, dxb, sq = _loss_head(cur, target)
    loss = lax.psum(0.5 * jnp.sum(sq) / d, ("x", "y", "c"))

    gs = {n: [None] * depth for n in SMALL}
    late = None
    half_d, half_ff = d // 2, d_ff // N_DEV // 2
    for l in reversed(range(depth)):
        sv = saved[l]
        gw_ff2, got = _mm_bwd_w("ff_down_dw", sv["r"], dxb, False,
                                jobs=[scatter(late[1], True, (half_d, half_d))] if late else [])
        if late:
            update("w_in", late[0], got[0], half_d)
        da1, parts = _mm_bwd_x("ff_down_dx", dxb, full["w_ff2"][l], tile_out(d_ff, BF16),
                               lambda acc, a1: (acc * (2.0 * jnp.maximum(a1, 0.0)),), tile_in(sv["a1"]),
                               jobs=[scatter(gw_ff2, False, (0, half_ff))])
        update("w_ff2", l, parts)
        gw_ff1, got = _mm_bwd_w("ff_up_dw", sv["h2"], da1, True, jobs=[scatter(gw_ff2, False, (half_ff, half_ff))])
        update("w_ff2", l, got[0], half_ff)
        dh2, parts = _mm_bwd_x("ff_up_dx", da1, full["w_ff1"][l], tile_out(d, F32),
                               jobs=[scatter(gw_ff1, True, (0, half_d))])
        update("w_ff1", l, parts)
        dx1, dx1b, dg = _rms_bwd("rms_ff_bwd", sv["x1"], g_ff3, l, dh2, dx)
        gs["g_ff"][l] = dg.reshape(d)

        gw_out = _mm_bwd_w("proj_out_dw", sv["merged"], dx1b, False)[0]
        dga, dgb, dya, dyb = _mm_bwd_x(
            "proj_out_dx", dx1b, full["w_out"][l], lambda tm, tn: tile_out(d, BF16)(tm, tn) * 4, _merge_bwd_epilogue,
            lambda tm, tn: [(a, (tm, tn), lambda i, j, k: (i, j)) for a in (sv["ya"], sv["yb"])]
            + [gate_tile(sv["proj"], off, tm, tn) for off in (gate_off, gate_off + d)], tn=512)
        gw_oa = _mm_bwd_w("oa_dw", sv["o"], dya, True)[0]
        gw_ob = _mm_bwd_w("ob_dw", sv["sg"], dyb, True)[0]
        do = _mm_bwd_x("oa_dx", dya, full["w_oa"][l], tile_out(sbw, BF16))[0]
        dsg = _mm_bwd_x("ob_dx", dyb, full["w_ob"][l], tile_out(sgw, F32))[0]

        dqn, dkn, dv, *parts = _attn_bwd(sv["qn"], sv["kn"], sv["vb"], do,
                                         jobs=[scatter(gw_ff1, True, (half_d, half_d)), scatter(gw_out, False),
                                               scatter(gw_oa, True), scatter(gw_ob, True)])
        update("w_ff1", l, parts[0], half_d)
        for n, p in zip(("w_out", "w_oa", "w_ob"), parts[1:]):
            update(n, l, p)
        dq, dk, dvb, dgq, dgk = _qk_bwd(sv["proj"], gq3, gk3, l, sbw, dqn, dkn, dv)
        gs["g_q"][l], gs["g_k"][l] = dgq.reshape(g_q.shape[1:]), dgk.reshape(g_k.shape[1:])
        du, dvs, dlng, dlnb, dwsp, dbsp = _sgu_bwd(sv["proj"], ln_g3, ln_b3, w_spatial, b_col, l, sbw, sgw, dsg)
        gs["sgu_ln_g"][l], gs["sgu_ln_b"][l] = dlng.reshape(sgw), dlnb.reshape(sgw)
        gs["w_spatial"][l], gs["b_spatial"][l] = dwsp, dbsp[:, :, 0]

        dproj = jnp.concatenate([dq, dk, dvb, du, dvs, dga, dgb], axis=1)
        gw_in = _mm_bwd_w("proj_in_dw", sv["h"], dproj, True)[0]
        late = (l, gw_in) if l > 0 else None
        dh, *parts = _mm_bwd_x("proj_in_dx", dproj, full["w_in"][l], tile_out(d, F32),
                               jobs=[scatter(gw_in, True, (0, half_d))] + ([] if late else [scatter(gw_in, True, (half_d, half_d))]))
        update("w_in", l, parts[0])
        if not late:
            update("w_in", l, parts[1], half_d)
        dx, dxb, dg = _rms_bwd("rms_mix_bwd", sv["x"], g_mix3, l, dh, dx1)
        gs["g_mix"][l] = dg.reshape(d)

    small_parts = _all_gather("gather_small", _pack([jnp.stack(gs[n]) for n in SMALL]))
    packed = _adam_small(small_parts, _pack([w[n] for n in SMALL]), _pack([mom[n] for n in SMALL]),
                         _pack([var[n] for n in SMALL]))
    unpacked = [_unpack(p, [w[n] for n in SMALL]) for p in packed]
    for i, n in enumerate(SMALL):
        res[n] = tuple(u[i] for u in unpacked)

    outs = [loss, dx[None]]
    for which in range(4):
        outs += [res[n][which] for n in ORDER]
    return tuple(outs)
```

```python
import functools

import jax
import jax.numpy as jnp
from jax import lax
from jax.experimental import pallas as pl
from jax.experimental.pallas import tpu as pltpu

F32 = jnp.float32
BF16 = jnp.bfloat16
SDS = jax.ShapeDtypeStruct
MESH = pl.DeviceIdType.MESH

N_DEV = 8
HEAD_DIM = 128
GROUP_DIM = 128
SGU_LEN = 128
SGU_CAUSAL = 64
NORM_EPS = 1e-6
ATT_BLOCK = 128
ATT_DEAD = -110.0
ATT_CHAINS = 4

ADAM_LR, ADAM_B1, ADAM_B2, ADAM_EPS, ADAM_WD, ADAM_STEP = 0.001, 0.9, 0.999, 1e-08, 0.01, 10

ROW_TILE = 512
MM_TM, MM_TN, MM_TK = 1024, 1536, 2304


def _tile(n, target, align=128):
    best = None
    for t in range(align, min(n, target) + 1, align):
        if n % t == 0:
            best = t
    return n if best is None else best


class _Exchange:
    def __init__(self, src, kind, rows=None, into=None):
        self.src, self.kind, self.rows, self.into = src, kind, rows, into
        n = rows[1] if rows else src.shape[-2]
        width = src.shape[-1]
        shape = {"rows": (N_DEV,) + src.shape, "cols": (src.shape[0], N_DEV * width),
                 "blocks": (N_DEV, n, width), "slabs": (N_DEV, n, width // N_DEV)}[kind]
        self.out = SDS(shape, src.dtype)
        self.sems = [pltpu.SemaphoreType.DMA((N_DEV - 1,)), pltpu.SemaphoreType.DMA((N_DEV - 1,)),
                     pltpu.SemaphoreType.DMA(())]

    def plan(self, src_ref, out_ref, send_sems, recv_sems, local_sem):
        x, y, c = _position()
        me = 4 * x + 2 * y + c
        part = pl.ds(*self.rows) if self.rows else slice(None)
        width = self.out.shape[-1] // N_DEV if self.kind == "cols" else self.out.shape[-1]
        slab = lambda q: pl.ds(q * width, width)
        if self.kind == "rows":
            src_of, dst_of = (lambda p: src_ref.at[part]), (lambda q: out_ref.at[q, part])
        elif self.kind == "cols":
            src_of, dst_of = (lambda p: src_ref.at[part]), (lambda q: out_ref.at[part, slab(q)])
        elif self.kind == "blocks":
            src_of, dst_of = (lambda p: src_ref.at[p, part]), (lambda q: out_ref.at[q])
        else:
            src_of, dst_of = (lambda p: src_ref.at[part, slab(p)]), (lambda q: out_ref.at[q])
        local = pltpu.make_async_copy(src_of(me), dst_of(me), local_sem)
        sends, arrivals = [], []
        for k in range(1, N_DEV):
            px = 1 - x if k & 4 else x
            py = 1 - y if k & 2 else y
            pc = 1 - c if k & 1 else c
            peer = 4 * px + 2 * py + pc
            for dst, keep in ((dst_of(me), sends), (dst_of(peer), arrivals)):
                keep.append(pltpu.make_async_remote_copy(
                    src_ref=src_of(peer), dst_ref=dst, send_sem=send_sems.at[k - 1], recv_sem=recv_sems.at[k - 1],
                    device_id=(px, py, pc), device_id_type=MESH))
        return local, sends, arrivals


def _call(name, grid, ins, outs, body, scratch=(), semantics=None, jobs=(), aliases=None):
    n_in, n_out, n_scr, n_job = len(ins), len(outs), len(scratch), len(jobs)
    any_spec = pl.BlockSpec(memory_space=pl.ANY)
    aliases = dict(aliases or {})
    completed = [(n, jb.into) for n, jb in enumerate(jobs) if jb.into is not None]
    for pos, (n, _) in enumerate(completed):
        aliases[n_in + n_job + pos] = n_out + n

    def wrapped(*refs):
        own_in, job_in = refs[:n_in], refs[n_in:n_in + n_job]
        p = n_in + n_job + len(completed)
        own_out, job_out = refs[p:p + n_out], refs[p + n_out:p + n_out + n_job]
        p += n_out + n_job
        own_scr, job_sems = refs[p:p + n_scr], refs[p + n_scr:]
        if not jobs:
            return body(*own_in, *own_out, *own_scr)
        ids = [pl.program_id(a) for a in range(len(grid))]
        first = functools.reduce(jnp.logical_and, [i == 0 for i in ids])
        last = functools.reduce(jnp.logical_and, [i == g - 1 for i, g in zip(ids, grid)])
        plans = [jb.plan(job_in[n], job_out[n], *job_sems[3 * n:3 * n + 3]) for n, jb in enumerate(jobs)]

        @pl.when(first)
        def _():
            for local, sends, _ in plans:
                local.start()
                for cp in sends:
                    cp.start()

        body(*own_in, *own_out, *own_scr)

        @pl.when(last)
        def _():
            for local, sends, arrivals in plans:
                for cp in arrivals:
                    cp.wait_recv()
                for cp in sends:
                    cp.wait_send()
                local.wait()

    spec = lambda blk, im: any_spec if blk is None else pl.BlockSpec(blk, im)
    res = pl.pallas_call(
        wrapped, name=name, grid=grid,
        in_specs=[spec(blk, im) for _, blk, im in ins] + [any_spec] * (n_job + len(completed)),
        out_specs=[spec(blk, im) for _, blk, im in outs] + [any_spec] * n_job,
        out_shape=[s for s, _, _ in outs] + [jb.out for jb in jobs],
        scratch_shapes=list(scratch) + [sem for jb in jobs for sem in jb.sems],
        input_output_aliases=aliases,
        compiler_params=pltpu.CompilerParams(
            dimension_semantics=("arbitrary",) * len(grid) if jobs or not semantics else semantics),
    )(*[a for a, _, _ in ins], *[jb.src for jb in jobs], *[into for _, into in completed])
    return res


def _dot(a, b, dims=((1,), (0,))):
    return lax.dot_general(a, b, (dims, ((), ())), preferred_element_type=F32)


NN = ((1,), (0,))
NT = ((1,), (1,))
TN = ((0,), (0,))


def _matmul(name, grid, a, b, dims, outs, epilogue=None, extras=(), jobs=()):
    nk = grid[2]
    n_ex, n_out = len(extras), len(outs)
    acc_shape = tuple(d for d in outs[0][1] if d is not None)

    def body(*refs):
        a_ref, b_ref = refs[0], refs[1]
        ex_refs = refs[2:2 + n_ex]
        out_refs = refs[2 + n_ex:2 + n_ex + n_out]
        def product():
            return _dot(a_ref[...].astype(BF16), b_ref[...].astype(BF16), dims)

        def finish(acc):
            vals = (acc,) if epilogue is None else epilogue(acc, *[r[...] for r in ex_refs])
            for r, v in zip(out_refs, vals):
                r[...] = v.astype(r.dtype)

        if nk == 1:
            finish(product())
        else:
            acc_ref = refs[-1]
            k = pl.program_id(2)

            @pl.when(k == 0)
            def _():
                acc_ref[...] = jnp.zeros_like(acc_ref)

            @pl.when(k < nk - 1)
            def _():
                acc_ref[...] += product()

            @pl.when(k == nk - 1)
            def _():
                finish(acc_ref[...] + product())

    return _call(name, grid, [a, b, *extras], list(outs), body,
                 scratch=[pltpu.VMEM(acc_shape, F32)] if nk > 1 else [],
                 semantics=("parallel", "parallel", "arbitrary"), jobs=jobs)


def _whole(gathered, col_sharded):
    if col_sharded:
        return jnp.transpose(gathered, (1, 0, 2)).reshape(gathered.shape[1], -1)
    return gathered.reshape(-1, gathered.shape[2])


def _n_tile(n, target):
    return _tile(n, MM_TN if n % MM_TN == 0 else target)


def _mm_fwd(name, a, w, outs_fn, epilogue=None, extras_fn=None, tn=1024, jobs=()):
    m, k_dim = a.shape
    n_out = w.shape[1]
    tm, tn, tk = _tile(m, MM_TM, 8), _n_tile(n_out, tn), _tile(k_dim, MM_TK)
    grid = (m // tm, n_out // tn, k_dim // tk)
    extras = extras_fn(tm, tn) if extras_fn else ()
    return _matmul(name, grid, (a, (tm, tk), lambda i, j, k: (i, k)), (w, (tk, tn), lambda i, j, k: (k, j)), NN,
                   outs_fn(tm, tn), epilogue, extras, jobs)


def _mm_bwd_x(name, a, w, outs_fn, epilogue=None, extras_fn=None, tn=1024, jobs=()):
    m, k_dim = a.shape
    n_out = w.shape[0]
    tm, tn, tk = _tile(m, MM_TM, 8), _tile(n_out, tn), _tile(k_dim, MM_TK)
    grid = (m // tm, n_out // tn, k_dim // tk)
    extras = extras_fn(tm, tn) if extras_fn else ()
    return _matmul(name, grid, (a, (tm, tk), lambda i, j, k: (i, k)), (w, (tn, tk), lambda i, j, k: (j, k)), NT,
                   outs_fn(tm, tn), epilogue, extras, jobs)


def _mm_bwd_w(name, a, b, col_sharded, jobs=()):
    m, ka = a.shape
    n = b.shape[1]
    ts = _tile(m, 2 * MM_TM, 8)
    tj = _tile(n, 1024)
    if col_sharded:
        ti = _tile(ka, 1024)
        out = (SDS((ka, n), BF16), (ti, tj), lambda i, j, k: (i, j))
    else:
        shard = ka // N_DEV
        ti = _tile(shard, 1024)
        ips = shard // ti
        out = (SDS((N_DEV, shard, n), BF16), (None, ti, tj), lambda i, j, k: (i // ips, i % ips, j))
    grad, *got = _matmul(name, (ka // ti, n // tj, m // ts), (a, (ts, ti), lambda i, j, k: (k, i)),
                         (b, (ts, tj), lambda i, j, k: (k, j)), TN, [out], jobs=jobs)
    return grad, got


def _rms_fwd(name, x, g3, l):
    s, d = x.shape
    ts = _tile(s, ROW_TILE, 8)

    def body(x_ref, g_ref, h_ref):
        xv = x_ref[...]
        r = lax.rsqrt(jnp.mean(xv * xv, axis=-1, keepdims=True) + NORM_EPS)
        h_ref[...] = (xv * r * g_ref[...]).astype(BF16)

    return _call(name, (s // ts,), [(x, (ts, d), lambda i: (i, 0)), (g3, (None, 1, d), lambda i: (l, 0, 0))],
                 [(SDS((s, d), BF16), (ts, d), lambda i: (i, 0))], body, semantics=("parallel",))[0]


def _rms_bwd(name, x, g3, l, dh, dres):
    s, d = x.shape
    ts = _tile(s, ROW_TILE, 8)

    def body(x_ref, g_ref, dh_ref, dres_ref, dx_ref, dxb_ref, dg_ref):
        xv = x_ref[...]
        r = lax.rsqrt(jnp.mean(xv * xv, axis=-1, keepdims=True) + NORM_EPS)
        xhat = xv * r
        dhv = dh_ref[...]
        gy = dhv * g_ref[...]
        m = jnp.mean(gy * xhat, axis=-1, keepdims=True)
        dx = dres_ref[...] + r * (gy - xhat * m)
        dx_ref[...] = dx
        dxb_ref[...] = dx.astype(BF16)
        part = jnp.sum(dhv * xhat, axis=0, keepdims=True)

        @pl.when(pl.program_id(0) == 0)
        def _():
            dg_ref[...] = part

        @pl.when(pl.program_id(0) > 0)
        def _():
            dg_ref[...] += part

    row = lambda i: (i, 0)
    return _call(name, (s // ts,),
                 [(x, (ts, d), row), (g3, (None, 1, d), lambda i: (l, 0, 0)), (dh, (ts, d), row), (dres, (ts, d), row)],
                 [(SDS((s, d), F32), (ts, d), row), (SDS((s, d), BF16), (ts, d), row),
                  (SDS((1, d), F32), (1, d), lambda i: (0, 0))], body)


def _loss_head(y, target):
    s, d = y.shape
    ts = _tile(s, ROW_TILE, 8)

    def body(y_ref, t_ref, dy_ref, dyb_ref, sq_ref):
        diff = y_ref[...] - t_ref[...]
        dy = diff / d
        dy_ref[...] = dy
        dyb_ref[...] = dy.astype(BF16)
        part = jnp.sum(diff * diff, axis=0, keepdims=True)

        @pl.when(pl.program_id(0) == 0)
        def _():
            sq_ref[...] = part

        @pl.when(pl.program_id(0) > 0)
        def _():
            sq_ref[...] += part

    row = lambda i: (i, 0)
    return _call("loss_head", (s // ts,), [(y, (ts, d), row), (target, (ts, d), row)],
                 [(SDS((s, d), F32), (ts, d), row), (SDS((s, d), BF16), (ts, d), row),
                  (SDS((1, d), F32), (1, d), lambda i: (0, 0))], body)


def _qkv_prep(proj, gq3, gk3, l, sbw):
    s = proj.shape[0]
    ts = _tile(s, ROW_TILE, 8)
    heads = sbw // HEAD_DIM

    def body(q_ref, k_ref, v_ref, gq_ref, gk_ref, qn_ref, kn_ref, vb_ref):
        for h in range(heads):
            sl = slice(h * HEAD_DIM, (h + 1) * HEAD_DIM)
            for src, g_ref, dst in ((q_ref, gq_ref, qn_ref), (k_ref, gk_ref, kn_ref)):
                t = src[:, sl]
                r = lax.rsqrt(jnp.mean(t * t, axis=-1, keepdims=True) + NORM_EPS)
                dst[:, sl] = (t * r * g_ref[:, sl]).astype(BF16)
        vb_ref[...] = v_ref[...].astype(BF16)

    gspec = lambda g: (g, (None, 1, sbw), lambda i: (l, 0, 0))
    col = lambda c: (proj, (ts, sbw), lambda i: (i, c))
    out = (SDS((s, sbw), BF16), (ts, sbw), lambda i: (i, 0))
    return _call("qkv_prep", (s // ts,), [col(0), col(1), col(2), gspec(gq3), gspec(gk3)], [out, out, out], body,
                 semantics=("parallel",))


def _qk_bwd(proj, gq3, gk3, l, sbw, dqn, dkn, dv):
    s = proj.shape[0]
    ts = _tile(s, ROW_TILE, 8)
    heads = sbw // HEAD_DIM

    def body(q_ref, k_ref, gq_ref, gk_ref, dqn_ref, dkn_ref, dv_ref, dq_ref, dk_ref, dvb_ref, dgq_ref, dgk_ref):
        first = pl.program_id(0) == 0
        for src, g_ref, dn_ref, dst, dg_ref in ((q_ref, gq_ref, dqn_ref, dq_ref, dgq_ref),
                                                (k_ref, gk_ref, dkn_ref, dk_ref, dgk_ref)):
            for h in range(heads):
                sl = slice(h * HEAD_DIM, (h + 1) * HEAD_DIM)
                t = src[:, sl]
                r = lax.rsqrt(jnp.mean(t * t, axis=-1, keepdims=True) + NORM_EPS)
                xhat = t * r
                dn = dn_ref[:, sl]
                gy = dn * g_ref[:, sl]
                m = jnp.mean(gy * xhat, axis=-1, keepdims=True)
                dst[:, sl] = (r * (gy - xhat * m)).astype(BF16)
                part = jnp.sum(dn * xhat, axis=0, keepdims=True)

                @pl.when(first)
                def _():
                    dg_ref[:, sl] = part

                @pl.when(jnp.logical_not(first))
                def _():
                    dg_ref[:, sl] += part
        dvb_ref[...] = dv_ref[...].astype(BF16)

    gspec = lambda g: (g, (None, 1, sbw), lambda i: (l, 0, 0))
    col = lambda c: (proj, (ts, sbw), lambda i: (i, c))
    row = lambda a: (a, (ts, sbw), lambda i: (i, 0))
    outb = (SDS((s, sbw), BF16), (ts, sbw), lambda i: (i, 0))
    outg = (SDS((1, sbw), F32), (1, sbw), lambda i: (0, 0))
    return _call("qk_bwd", (s // ts,), [col(0), col(1), gspec(gq3), gspec(gk3), row(dqn), row(dkn), row(dv)],
                 [outb, outb, outb, outg, outg], body)


def _softplus(z):
    return jnp.maximum(z, 0.0) + jnp.log1p(jnp.exp(-jnp.abs(z)))


def _tri_dot(x, tri):
    hi = x.astype(BF16)
    lo = (x - hi.astype(F32)).astype(BF16)
    return _dot(hi, tri) + _dot(lo, tri)


def _att_common(tb):
    rows = lax.broadcasted_iota(jnp.int32, (tb, tb), 0)
    cols = lax.broadcasted_iota(jnp.int32, (tb, tb), 1)
    suffix = (rows >= cols).astype(BF16)
    prefix = (rows <= cols).astype(BF16)
    return rows, cols, suffix, prefix, jnp.ones((tb, tb), BF16)


def _key_norm_max(k_ref, kmax_ref):
    kf = k_ref[...].astype(F32)
    n2 = jnp.sum(kf * kf, axis=1, keepdims=True)
    kmax_ref[...] = jnp.broadcast_to(jnp.max(n2, axis=0, keepdims=True), kmax_ref.shape)


def _att_reach(q, kmax_ref, scale):
    qf = q.astype(F32)
    qn2 = jnp.sum(qf * qf, axis=1, keepdims=True)
    return scale * jnp.sqrt(qn2 * kmax_ref[0:1, 0:1]) * 1.001 + 1e-3


def _att_chains(s, tb):
    return ATT_CHAINS if s % (ATT_CHAINS * tb) == 0 else 1


def _att_block(diag, j, tb, rows, cols):
    kb = diag - j
    off = pl.multiple_of(jnp.maximum(kb, 0) * tb, tb)
    limit = jnp.where(kb >= 0, diag * tb, -(1 << 30))
    return off, (cols + kb * tb) < (rows + limit)


def _att_blocks(diag, j, tb, rows, cols, masked):
    if masked:
        return zip(*[_att_block(dg, j, tb, rows, cols) for dg in diag])
    return [pl.multiple_of((dg - j) * tb, tb) for dg in diag], None


def _att_sweeps(sweep, diag):
    alive = lambda last: lambda carry: jnp.logical_and(carry[0] <= last, carry[1])
    carry = sweep(True)((jnp.int32(0), jnp.bool_(True)))
    carry = lax.while_loop(alive(diag[0]), sweep(False), carry)
    return lax.while_loop(alive(diag[-1]), sweep(True), carry)[0]


def _attn_fwd(qn, kn, vb, jobs=()):
    s, sbw = qn.shape
    heads = sbw // HEAD_DIM
    tb = _tile(s, ATT_BLOCK, 8)
    nch = _att_chains(s, tb)
    tq = nch * tb
    scale = HEAD_DIM ** -0.5

    def body(q_ref, k_ref, v_ref, o_ref, kmax_ref, reach_ref, run_ref, acc_ref):
        qi = pl.program_id(1)

        @pl.when(qi == 0)
        def _():
            _key_norm_max(k_ref, kmax_ref)

        rows, cols, suffix, _, ones = _att_common(tb)
        tri = jnp.concatenate([suffix, ones], axis=1)
        diag = [qi * nch + c for c in range(nch)]
        for c in range(nch):
            q = q_ref[c * tb:(c + 1) * tb, :]
            reach_ref[c] = jnp.broadcast_to(_att_reach(q, kmax_ref, scale), (tb, tb))
        run_ref[...] = jnp.zeros_like(run_ref)
        acc_ref[...] = jnp.zeros_like(acc_ref)

        def sweep(masked):
            def step(carry):
                j, alive, chains = carry[0], None, range(nch)
                offs, pasts = _att_blocks(diag, j, tb, rows, cols, masked)
                hide = (lambda c, val: jnp.where(pasts[c], val, 0.0)) if masked else (lambda c, val: val)
                runs = [run_ref[c] for c in chains]
                zs = [_dot(q_ref[c * tb:(c + 1) * tb, :], k_ref[pl.ds(offs[c], tb), :], NT) * scale for c in chains]
                keeps = [hide(c, -_softplus(zs[c])) for c in chains]
                sums = [_tri_dot(keeps[c], tri) for c in chains]
                probs = [hide(c, jnp.exp(zs[c] + sums[c][:, :tb] + runs[c])).astype(BF16) for c in chains]
                outs = [_dot(probs[c], v_ref[pl.ds(offs[c], tb), :]) for c in chains]
                for c in chains:
                    acc_ref[c] += outs[c]
                    run = runs[c] + sums[c][:, tb:]
                    run_ref[c] = run
                    m = jnp.max(run + reach_ref[c])
                    alive = m if alive is None else jnp.maximum(alive, m)
                return j + 1, alive > ATT_DEAD
            return step

        _att_sweeps(sweep, diag)
        for c in range(nch):
            o_ref[c * tb:(c + 1) * tb, :] = acc_ref[c].astype(o_ref.dtype)

    qspec = lambda a: (a, (tq, HEAD_DIM), lambda h, i: (i, h))
    full = lambda a: (a, (s, HEAD_DIM), lambda h, i: (0, h))
    state = pltpu.VMEM((nch, tb, tb), F32)
    return _call("attn_fwd", (heads, s // tq), [qspec(qn), full(kn), full(vb)],
                 [(SDS((s, sbw), BF16), (tq, HEAD_DIM), lambda h, i: (i, h))], body,
                 scratch=[pltpu.VMEM((8, 128), F32), state, state, pltpu.VMEM((nch, tb, HEAD_DIM), F32)],
                 semantics=("parallel", "arbitrary"), jobs=jobs)


def _attn_bwd(qn, kn, vb, do, jobs=()):
    s, sbw = qn.shape
    heads = sbw // HEAD_DIM
    tb = _tile(s, ATT_BLOCK, 8)
    nch = _att_chains(s, tb)
    tq = nch * tb
    scale = HEAD_DIM ** -0.5

    def body(q_ref, k_ref, v_ref, do_ref, dq_ref, dk_ref, dv_ref, kmax_ref, reach_ref, run_ref, g_ref):
        qi = pl.program_id(1)

        @pl.when(qi == 0)
        def _():
            _key_norm_max(k_ref, kmax_ref)
            dk_ref[...] = jnp.zeros_like(dk_ref)
            dv_ref[...] = jnp.zeros_like(dv_ref)

        rows, cols, suffix, prefix, ones = _att_common(tb)
        tri_back = jnp.concatenate([suffix, ones], axis=1)
        tri_fwd = jnp.concatenate([prefix, ones], axis=1)
        diag = [qi * nch + c for c in range(nch)]
        rows_of = lambda ref, c: ref[c * tb:(c + 1) * tb, :]
        for c in range(nch):
            reach_ref[c] = jnp.broadcast_to(_att_reach(rows_of(q_ref, c), kmax_ref, scale), (tb, tb))
        run_ref[...] = jnp.zeros_like(run_ref)
        dq_ref[...] = jnp.zeros_like(dq_ref)

        chains = range(nch)

        def sweep_back(masked):
            def step(carry):
                j, alive = carry[0], None
                offs, pasts = _att_blocks(diag, j, tb, rows, cols, masked)
                hide = (lambda c, val: jnp.where(pasts[c], val, 0.0)) if masked else (lambda c, val: val)
                runs = [run_ref[c] for c in chains]
                zs = [_dot(rows_of(q_ref, c), k_ref[pl.ds(offs[c], tb), :], NT) * scale for c in chains]
                das = [_dot(rows_of(do_ref, c), v_ref[pl.ds(offs[c], tb), :], NT) for c in chains]
                keeps = [hide(c, -_softplus(zs[c])) for c in chains]
                sums = [_tri_dot(keeps[c], tri_back) for c in chains]
                probs = [hide(c, jnp.exp(zs[c] + sums[c][:, :tb] + runs[c])) for c in chains]
                dvs = [_dot(probs[c].astype(BF16), rows_of(do_ref, c), TN) for c in chains]
                for c in chains:
                    g_ref[c, j] = probs[c] * das[c]
                    dv_ref[pl.ds(offs[c], tb), :] += dvs[c]
                    run = runs[c] + sums[c][:, tb:]
                    run_ref[c] = run
                    m = jnp.max(run + reach_ref[c])
                    alive = m if alive is None else jnp.maximum(alive, m)
                return j + 1, alive > ATT_DEAD
            return step

        nsteps = _att_sweeps(sweep_back, diag)
        run_ref[...] = jnp.zeros_like(run_ref)

        def sweep_fwd(i, carry):
            j = nsteps - 1 - i
            offs, pasts = zip(*[_att_block(diag[c], j, tb, rows, cols) for c in chains])
            runs = [run_ref[c] for c in chains]
            ks = [k_ref[pl.ds(offs[c], tb), :] for c in chains]
            sgs = [jax.nn.sigmoid(_dot(rows_of(q_ref, c), ks[c], NT) * scale) for c in chains]
            gs = [g_ref[c, j] for c in chains]
            sums = [_tri_dot(gs[c], tri_fwd) for c in chains]
            dzs = [(jnp.where(pasts[c], gs[c] - sgs[c] * (runs[c] + sums[c][:, :tb]), 0.0) * scale).astype(BF16)
                   for c in chains]
            dqs = [_dot(dzs[c], ks[c]) for c in chains]
            dks = [_dot(dzs[c], rows_of(q_ref, c), TN) for c in chains]
            for c in chains:
                dq_ref[c * tb:(c + 1) * tb, :] += dqs[c]
                dk_ref[pl.ds(offs[c], tb), :] += dks[c]
                run_ref[c] = runs[c] + sums[c][:, tb:]
            return carry

        lax.fori_loop(0, nsteps, sweep_fwd, 0)

    qspec = lambda a: (a, (tq, HEAD_DIM), lambda h, i: (i, h))
    full = lambda a: (a, (s, HEAD_DIM), lambda h, i: (0, h))
    outq = (SDS((s, sbw), F32), (tq, HEAD_DIM), lambda h, i: (i, h))
    outf = (SDS((s, sbw), F32), (s, HEAD_DIM), lambda h, i: (0, h))
    state = pltpu.VMEM((nch, tb, tb), F32)
    return _call("attn_bwd", (heads, s // tq), [qspec(qn), full(kn), full(vb), qspec(do)], [outq, outf, outf], body,
                 scratch=[pltpu.VMEM((8, 128), F32), state, state, pltpu.VMEM((nch, s // tb, tb, tb), F32)],
                 semantics=("parallel", "arbitrary"), jobs=jobs)


_SQRT_HALF = 0.7071067811865476
_INV_SQRT_2PI = 0.3989422804014327


def _gelu(x):
    return 0.5 * x * (1.0 + lax.erf(x * _SQRT_HALF))


def _gelu_grad(x):
    return 0.5 * (1.0 + lax.erf(x * _SQRT_HALF)) + x * jnp.exp(-0.5 * x * x) * _INV_SQRT_2PI


def _sgu_mask():
    i = lax.broadcasted_iota(jnp.int32, (SGU_LEN, SGU_LEN), 0) // SGU_CAUSAL
    j = lax.broadcasted_iota(jnp.int32, (SGU_LEN, SGU_LEN), 1) // SGU_CAUSAL
    return j <= i


def _sgu_norm(vg, lng, lnb):
    mu = jnp.mean(vg, axis=-1, keepdims=True)
    xc = vg - mu
    rstd = lax.rsqrt(jnp.mean(xc * xc, axis=-1, keepdims=True) + NORM_EPS)
    xhat = xc * rstd
    return xhat, rstd, xhat * lng + lnb


def _sgu_specs(proj, ln_g3, ln_b3, w_sp, b_col, l, sbw, sgw, rb):
    groups = sgw // GROUP_DIM
    ucol, vcol = 3 * sbw // sgw, 3 * sbw // sgw + 1
    vec = lambda a: (a, (None, 1, sgw), lambda i: (l, 0, 0))
    mat = lambda a: (a, (None, groups, SGU_LEN, SGU_LEN), lambda i: (l, 0, 0, 0))
    return [(proj, (rb, sgw), lambda i: (i, ucol)), (proj, (rb, sgw), lambda i: (i, vcol)),
            vec(ln_g3), vec(ln_b3), mat(w_sp), mat(b_col)]


def _sgu_fwd(proj, ln_g3, ln_b3, w_sp, b_col, l, sbw, sgw):
    s = proj.shape[0]
    rb = _tile(s, 2 * SGU_LEN, SGU_LEN)
    groups = sgw // GROUP_DIM

    def body(u_ref, v_ref, lng_ref, lnb_ref, w_ref, b_ref, s_ref):
        mask = _sgu_mask()
        _, _, vln = _sgu_norm(_gelu(v_ref[...]), lng_ref[...], lnb_ref[...])
        vb = vln.astype(BF16)
        for g in range(groups):
            cs = slice(g * GROUP_DIM, (g + 1) * GROUP_DIM)
            wm = jnp.where(mask, w_ref[g], 0.0).astype(BF16)
            for c in range(rb // SGU_LEN):
                rs = slice(c * SGU_LEN, (c + 1) * SGU_LEN)
                mixed = _dot(wm, vb[rs, cs]) + b_ref[g]
                s_ref[rs, cs] = (_gelu(u_ref[rs, cs]) * mixed).astype(BF16)

    return _call("sgu_fwd", (s // rb,), _sgu_specs(proj, ln_g3, ln_b3, w_sp, b_col, l, sbw, sgw, rb),
                 [(SDS((s, sgw), BF16), (rb, sgw), lambda i: (i, 0))], body, semantics=("parallel",))[0]


def _sgu_bwd(proj, ln_g3, ln_b3, w_sp, b_col, l, sbw, sgw, ds):
    s = proj.shape[0]
    rb = _tile(s, 2 * SGU_LEN, SGU_LEN)
    groups = sgw // GROUP_DIM
    nsteps = s // rb

    def body(u_ref, v_ref, lng_ref, lnb_ref, w_ref, b_ref, ds_ref,
             du_ref, dv_ref, dlng_ref, dlnb_ref, dw_ref, db_ref, dvln_ref, dw_acc, db_acc):
        step = pl.program_id(0)

        @pl.when(step == 0)
        def _():
            dw_acc[...] = jnp.zeros_like(dw_acc)
            db_acc[...] = jnp.zeros_like(db_acc)

        mask = _sgu_mask()
        vp = v_ref[...]
        lng = lng_ref[...]
        xhat, rstd, vln = _sgu_norm(_gelu(vp), lng, lnb_ref[...])
        vb = vln.astype(BF16)
        for g in range(groups):
            cs = slice(g * GROUP_DIM, (g + 1) * GROUP_DIM)
            wm = jnp.where(mask, w_ref[g], 0.0).astype(BF16)
            for c in range(rb // SGU_LEN):
                rs = slice(c * SGU_LEN, (c + 1) * SGU_LEN)
                vbp = vb[rs, cs]
                mixed = _dot(wm, vbp) + b_ref[g]
                up = u_ref[rs, cs]
                dsp = ds_ref[rs, cs]
                du_ref[rs, cs] = (dsp * mixed * _gelu_grad(up)).astype(BF16)
                dmixed = dsp * _gelu(up)
                dmb = dmixed.astype(BF16)
                dvln_ref[rs, cs] = _dot(wm, dmb, TN)
                dw_acc[g] += _dot(dmb, vbp, NT)
                db_acc[g] += dmixed
        dvln = dvln_ref[...]
        part_g = jnp.sum(dvln * xhat, axis=0, keepdims=True)
        part_b = jnp.sum(dvln, axis=0, keepdims=True)

        @pl.when(step == 0)
        def _():
            dlng_ref[...] = part_g
            dlnb_ref[...] = part_b

        @pl.when(step > 0)
        def _():
            dlng_ref[...] += part_g
            dlnb_ref[...] += part_b

        dxhat = dvln * lng
        m1 = jnp.mean(dxhat, axis=-1, keepdims=True)
        m2 = jnp.mean(dxhat * xhat, axis=-1, keepdims=True)
        dv_ref[...] = (rstd * (dxhat - m1 - xhat * m2) * _gelu_grad(vp)).astype(BF16)

        @pl.when(step == nsteps - 1)
        def _():
            for g in range(groups):
                dw_ref[g] = jnp.where(mask, dw_acc[g], 0.0)
                db_ref[g] = jnp.broadcast_to(jnp.sum(db_acc[g], axis=-1, keepdims=True), (SGU_LEN, SGU_LEN))

    row = lambda i: (i, 0)
    outb = (SDS((s, sgw), BF16), (rb, sgw), row)
    outv = (SDS((1, sgw), F32), (1, sgw), lambda i: (0, 0))
    outm = (SDS((groups, SGU_LEN, SGU_LEN), F32), (groups, SGU_LEN, SGU_LEN), lambda i: (0, 0, 0))
    acc = pltpu.VMEM((groups, SGU_LEN, SGU_LEN), F32)
    return _call("sgu_bwd", (nsteps,),
                 _sgu_specs(proj, ln_g3, ln_b3, w_sp, b_col, l, sbw, sgw, rb) + [(ds, (rb, sgw), row)],
                 [outb, outb, outv, outv, outm, outm], body, scratch=[pltpu.VMEM((rb, sgw), F32), acc, acc])


def _merge_fwd(o, sg, w_oa, w_ob, proj, d, gate_off, jobs=()):
    s, kw = o.shape
    tm, tn = _tile(s, MM_TM, 8), _tile(d, 512)

    def body(o_ref, s_ref, wa_ref, wb_ref, ga_ref, gb_ref, ya_ref, yb_ref, mg_ref):
        ya = _dot(o_ref[...], wa_ref[...])
        yb = _dot(s_ref[...], wb_ref[...])
        ya_ref[...] = ya.astype(ya_ref.dtype)
        yb_ref[...] = yb.astype(yb_ref.dtype)
        mg_ref[...] = (jax.nn.sigmoid(ga_ref[...]) * ya + jax.nn.sigmoid(gb_ref[...]) * yb).astype(BF16)

    act = lambda a: (a, (tm, kw), lambda i, j: (i, 0))
    wsp = lambda w: (w, (kw, tn), lambda i, j: (0, j))
    gate = lambda off: (proj, (tm, tn), lambda i, j: (i, off // tn + j))
    tile = lambda dt: (SDS((s, d), dt), (tm, tn), lambda i, j: (i, j))
    return _call("merge_fwd", (s // tm, d // tn), [act(o), act(sg), wsp(w_oa), wsp(w_ob), gate(gate_off), gate(gate_off + d)],
                 [tile(BF16), tile(BF16), tile(BF16)], body, semantics=("parallel", "parallel"), jobs=jobs)


def _merge_bwd_epilogue(dm, ya, yb, ga, gb):
    sa, sb = jax.nn.sigmoid(ga), jax.nn.sigmoid(gb)
    return dm * ya * sa * (1.0 - sa), dm * yb * sb * (1.0 - sb), dm * sa, dm * sb


def _position():
    x, y, c = lax.axis_index("x"), lax.axis_index("y"), lax.axis_index("c")
    return x, y, c


def _all_gather(name, shard):
    def body(x_ref, out_ref, send_sems, recv_sems, local_sem):
        x, y, c = _position()
        me, sibling = (x, y, c), (x, y, 1 - c)
        chips = [(1 - x, y), (x, 1 - y), (1 - x, 1 - y)]

        def slot(px, py, pc):
            return out_ref.at[4 * px + 2 * py + pc]

        def copy(k, block, to, src=None):
            return pltpu.make_async_remote_copy(
                src_ref=slot(*block) if src is None else src, dst_ref=slot(*block),
                send_sem=send_sems.at[k], recv_sem=recv_sems.at[k], device_id=to, device_id_type=MESH)

        mine = pltpu.make_async_copy(x_ref, slot(*me), local_sem)
        mine.start()
        first = [copy(0, me, sibling, src=x_ref)]
        first += [copy(1 + j, me, (*chip, c), src=x_ref) for j, chip in enumerate(chips)]
        for cp in first:
            cp.start()
        passed = [copy(4 + j, (*chip, c), sibling) for j, chip in enumerate(chips)]
        for j, chip in enumerate(chips):
            copy(1 + j, (*chip, c), me).wait_recv()
            passed[j].start()
        copy(0, sibling, me).wait_recv()
        for j, chip in enumerate(chips):
            copy(4 + j, (*chip, 1 - c), me).wait_recv()
        for cp in first + passed:
            cp.wait_send()
        mine.wait()

    return pl.pallas_call(
        body, name=name, out_shape=SDS((N_DEV,) + shard.shape, shard.dtype),
        in_specs=[pl.BlockSpec(memory_space=pl.ANY)], out_specs=pl.BlockSpec(memory_space=pl.ANY),
        scratch_shapes=[pltpu.SemaphoreType.DMA((7,)), pltpu.SemaphoreType.DMA((7,)), pltpu.SemaphoreType.DMA(())],
    )(shard)


def _adam_math(g, w, m, v):
    m = ADAM_B1 * m + (1.0 - ADAM_B1) * g
    v = ADAM_B2 * v + (1.0 - ADAM_B2) * (g * g)
    m_hat = m / (1.0 - ADAM_B1 ** ADAM_STEP)
    v_hat = v / (1.0 - ADAM_B2 ** ADAM_STEP)
    delta = -ADAM_LR * (m_hat / (jnp.sqrt(v_hat) + ADAM_EPS) + ADAM_WD * w)
    return delta, m, v


def _adam(name, parts, w, m, v, l, prev, row0=0):
    _, r, c = parts.shape
    tr, tc = _tile(r, 256, 8), _tile(c, MM_TN)
    rb = row0 // tr

    def body(p_ref, w_ref, m_ref, v_ref, *rest):
        g_ref, d_ref, mo_ref, vo_ref = rest[-4:]
        g = p_ref[0].astype(F32)
        for q in range(1, N_DEV):
            g = g + p_ref[q].astype(F32)
        delta, mn, vn = _adam_math(g, w_ref[...], m_ref[...], v_ref[...])
        g_ref[...] = g
        d_ref[...] = delta
        mo_ref[...] = mn
        vo_ref[...] = vn

    cur = lambda a: (a, (None, tr, tc), lambda i, j: (l, rb + i, j))
    out = (SDS(w.shape, F32), (None, tr, tc), lambda i, j: (l, rb + i, j))
    ins = [(parts, (N_DEV, tr, tc), lambda i, j: (0, i, j)), cur(w), cur(m), cur(v)]
    aliases = None
    if prev is not None:
        ins += [(p, None, None) for p in prev]
        aliases = {4 + n: n for n in range(4)}
    return _call(name, (r // tr, c // tc), ins, [out, out, out, out], body, semantics=("parallel", "parallel"),
                 aliases=aliases)


def _adam_small(parts, w, m, v):
    r = w.shape[0]
    tr = _tile(r, 1096, 8)

    def body(p_ref, w_ref, m_ref, v_ref, g_ref, d_ref, mo_ref, vo_ref):
        g = p_ref[0]
        for q in range(1, N_DEV):
            g = g + p_ref[q]
        delta, mn, vn = _adam_math(g, w_ref[...], m_ref[...], v_ref[...])
        g_ref[...] = g
        d_ref[...] = delta
        mo_ref[...] = mn
        vo_ref[...] = vn

    cur = lambda a: (a, (tr, 128), lambda i: (i, 0))
    out = (SDS(w.shape, F32), (tr, 128), lambda i: (i, 0))
    return _call("adam_small", (r // tr,), [(parts, (N_DEV, tr, 128), lambda i: (0, i, 0)), cur(w), cur(m), cur(v)],
                 [out, out, out, out], body, semantics=("parallel",))


SMALL = ("g_mix", "g_q", "g_k", "sgu_ln_g", "sgu_ln_b", "w_spatial", "b_spatial", "g_ff")
BIG = ("w_in", "w_oa", "w_ob", "w_out", "w_ff1", "w_ff2")
COL_SHARDED = ("w_in", "w_oa", "w_ob", "w_ff1")
ORDER = ("g_mix", "w_in", "g_q", "g_k", "sgu_ln_g", "sgu_ln_b", "w_spatial", "b_spatial", "w_oa", "w_ob", "w_out",
         "g_ff", "w_ff1", "w_ff2")


def _pack(arrs):
    return jnp.concatenate([a.reshape(-1) for a in arrs]).reshape(-1, 128)


def _unpack(packed, like):
    flat, out, off = packed.reshape(-1), [], 0
    for a in like:
        out.append(flat[off:off + a.size].reshape(a.shape))
        off += a.size
    return out


def kernel(x, g_mix, w_in, g_q, g_k, sgu_ln_g, sgu_ln_b, w_spatial, b_spatial, w_oa, w_ob, w_out, g_ff, w_ff1, w_ff2, loss_target, m_g_mix, m_w_in, m_g_q, m_g_k, m_sgu_ln_g, m_sgu_ln_b, m_w_spatial, m_b_spatial, m_w_oa, m_w_ob, m_w_out, m_g_ff, m_w_ff1, m_w_ff2, v_g_mix, v_w_in, v_g_q, v_g_k, v_sgu_ln_g, v_sgu_ln_b, v_w_spatial, v_b_spatial, v_w_oa, v_w_ob, v_w_out, v_g_ff, v_w_ff1, v_w_ff2):
    w = dict(g_mix=g_mix, w_in=w_in, g_q=g_q, g_k=g_k, sgu_ln_g=sgu_ln_g, sgu_ln_b=sgu_ln_b, w_spatial=w_spatial,
             b_spatial=b_spatial, w_oa=w_oa, w_ob=w_ob, w_out=w_out, g_ff=g_ff, w_ff1=w_ff1, w_ff2=w_ff2)
    mom = dict(g_mix=m_g_mix, w_in=m_w_in, g_q=m_g_q, g_k=m_g_k, sgu_ln_g=m_sgu_ln_g, sgu_ln_b=m_sgu_ln_b,
               w_spatial=m_w_spatial, b_spatial=m_b_spatial, w_oa=m_w_oa, w_ob=m_w_ob, w_out=m_w_out, g_ff=m_g_ff,
               w_ff1=m_w_ff1, w_ff2=m_w_ff2)
    var = dict(g_mix=v_g_mix, w_in=v_w_in, g_q=v_g_q, g_k=v_g_k, sgu_ln_g=v_sgu_ln_g, sgu_ln_b=v_sgu_ln_b,
               w_spatial=v_w_spatial, b_spatial=v_b_spatial, w_oa=v_w_oa, w_ob=v_w_ob, w_out=v_w_out, g_ff=v_g_ff,
               w_ff1=v_w_ff1, w_ff2=v_w_ff2)

    xs = x[0]
    target = loss_target[0]
    s, d = xs.shape
    depth = g_mix.shape[0]
    sbw = g_q.shape[1] * g_q.shape[2]
    sgw = sgu_ln_g.shape[1]
    d_ff =w_ff1.shape[2] * N_DEV
    n_in = w_in.shape[2] * N_DEV
    gate_off = 3 * sbw + 2 * sgw

    wb = {n: w[n].astype(BF16) for n in BIG}
    full = {n: [None] * depth for n in BIG}

    for n in ("w_in", "w_oa", "w_ob", "w_out"):
        full[n][0] = _whole(_all_gather("gather_" + n, wb[n][0]), n in COL_SHARDED)

    def gather(names, l, rows=None, into=None):
        if l >= depth:
            return []
        return [_Exchange(wb[n][l], "cols" if n in COL_SHARDED else "rows", rows, into) for n in names]

    def arrived(names, l, gathered):
        for n, g in zip(names, gathered):
            full[n][l] = g if n in COL_SHARDED else g.reshape(-1, g.shape[2])

    def scatter(grad, col_sharded, rows=None):
        return _Exchange(grad, "slabs" if col_sharded else "blocks", rows)

    res = {}

    def update(n, l, parts, row0=0):
        res[n] = _adam("adam_" + n, parts, w[n], mom[n], var[n], l, res.get(n), row0)

    g_mix3, g_ff3 = g_mix.reshape(depth, 1, d), g_ff.reshape(depth, 1, d)
    gq3, gk3 = g_q.reshape(depth, 1, sbw), g_k.reshape(depth, 1, sbw)
    ln_g3, ln_b3 = sgu_ln_g.reshape(depth, 1, sgw), sgu_ln_b.reshape(depth, 1, sgw)
    b_col = jnp.broadcast_to(b_spatial[..., None], b_spatial.shape + (SGU_LEN,))

    def tile_out(n, dt):
        return lambda tm, tn: [(SDS((s, n), dt), (tm, tn), lambda i, j, k: (i, j))]

    def tile_in(a):
        return lambda tm, tn: [(a, (tm, tn), lambda i, j, k: (i, j))]

    def gate_tile(proj, off, tm, tn):
        return (proj, (tm, tn), lambda i, j, k: (i, off // tn + j))

    saved = []
    cur = xs
    for l in range(depth):
        h = _rms_fwd("rms_mix", cur, g_mix3, l)
        proj, *got = _mm_fwd("proj_in", h, full["w_in"][l], tile_out(n_in, F32), jobs=gather(["w_ff1"], l))
        arrived(["w_ff1"], l, got)
        qn, kn, vb = _qkv_prep(proj, gq3, gk3, l, sbw)
        o, *got = _attn_fwd(qn, kn, vb, jobs=gather(["w_ff2"], l))
        arrived(["w_ff2"], l, got)
        sg = _sgu_fwd(proj, ln_g3, ln_b3, w_spatial, b_col, l, sbw, sgw)
        ya, yb, merged, *got = _merge_fwd(o, sg, full["w_oa"][l], full["w_ob"][l], proj, d, gate_off,
                                          jobs=gather(["w_oa", "w_ob"], l + 1))
        arrived(["w_oa", "w_ob"], l + 1, got)
        x1, *got = _mm_fwd("proj_out", merged, full["w_out"][l], tile_out(d, F32), lambda acc, res: (res + acc,),
                           tile_in(cur), jobs=gather(["w_out"], l + 1))
        arrived(["w_out"], l + 1, got)
        h2 = _rms_fwd("rms_ff", x1, g_ff3, l)
        half = d // 2
        a1, r, *got = _mm_fwd("ff_up", h2, full["w_ff1"][l],
                              lambda tm, tn: tile_out(d_ff, F32)(tm, tn) + tile_out(d_ff, BF16)(tm, tn),
                              lambda acc: (acc, jnp.square(jnp.maximum(acc, 0.0))),
                              jobs=gather(["w_in"], l + 1, (0, half)))
        x2, *got = _mm_fwd("ff_down", r, full["w_ff2"][l], tile_out(d, F32), lambda acc, res: (res + acc,),
                           tile_in(x1), jobs=gather(["w_in"], l + 1, (half, half), got[0] if got else None))
        arrived(["w_in"], l + 1, got)
        saved.append(dict(x=cur, h=h, proj=proj, qn=qn, kn=kn, vb=vb, o=o, sg=sg, ya=ya, yb=yb, merged=merged,
                          x1=x1, h2=h2, a1=a1, r=r))
        cur = x2

    dx, dxb, sq = _loss_head(cur, target)
    loss = lax.psum(0.5 * jnp.sum(sq) / d, ("x", "y", "c"))

    gs = {n: [None] * depth for n in SMALL}
    late = None
    half_d, half_ff = d // 2, d_ff // N_DEV // 2
    for l in reversed(range(depth)):
        sv = saved[l]
        gw_ff2, got = _mm_bwd_w("ff_down_dw", sv["r"], dxb, False,
                                jobs=[scatter(late[1], True, (half_d, half_d))] if late else [])
        if late:
            update("w_in", late[0], got[0], half_d)
        da1, parts = _mm_bwd_x("ff_down_dx", dxb, full["w_ff2"][l], tile_out(d_ff, BF16),
                               lambda acc, a1: (acc * (2.0 * jnp.maximum(a1, 0.0)),), tile_in(sv["a1"]),
                               jobs=[scatter(gw_ff2, False, (0, half_ff))])
        update("w_ff2", l, parts)
        gw_ff1, got = _mm_bwd_w("ff_up_dw", sv["h2"], da1, True, jobs=[scatter(gw_ff2, False, (half_ff, half_ff))])
        update("w_ff2", l, got[0], half_ff)
        dh2, parts = _mm_bwd_x("ff_up_dx", da1, full["w_ff1"][l], tile_out(d, F32),
                               jobs=[scatter(gw_ff1, True, (0, half_d))])
        update("w_ff1", l, parts)
        dx1, dx1b, dg = _rms_bwd("rms_ff_bwd", sv["x1"], g_ff3, l, dh2, dx)
        gs["g_ff"][l] = dg.reshape(d)

        gw_out = _mm_bwd_w("proj_out_dw", sv["merged"], dx1b, False)[0]
        dga, dgb, dya, dyb = _mm_bwd_x(
            "proj_out_dx", dx1b, full["w_out"][l], lambda tm, tn: tile_out(d, BF16)(tm, tn) * 4, _merge_bwd_epilogue,
            lambda tm, tn: [(a, (tm, tn), lambda i, j, k: (i, j)) for a in (sv["ya"], sv["yb"])]
            + [gate_tile(sv["proj"], off, tm, tn) for off in (gate_off, gate_off + d)], tn=512)
        gw_oa = _mm_bwd_w("oa_dw", sv["o"], dya, True)[0]
        gw_ob = _mm_bwd_w("ob_dw", sv["sg"], dyb, True)[0]
        do = _mm_bwd_x("oa_dx", dya, full["w_oa"][l], tile_out(sbw, BF16))[0]
        dsg = _mm_bwd_x("ob_dx", dyb, full["w_ob"][l], tile_out(sgw, F32))[0]

        dqn, dkn, dv, *parts = _attn_bwd(sv["qn"], sv["kn"], sv["vb"], do,
                                         jobs=[scatter(gw_ff1, True, (half_d, half_d)), scatter(gw_out, False),
                                               scatter(gw_oa, True), scatter(gw_ob, True)])
        update("w_ff1", l, parts[0], half_d)
        for n, p in zip(("w_out", "w_oa", "w_ob"), parts[1:]):
            update(n, l, p)
        dq, dk, dvb, dgq, dgk = _qk_bwd(sv["proj"], gq3, gk3, l, sbw, dqn, dkn, dv)
        gs["g_q"][l], gs["g_k"][l] = dgq.reshape(g_q.shape[1:]), dgk.reshape(g_k.shape[1:])
        du, dvs, dlng, dlnb, dwsp, dbsp = _sgu_bwd(sv["proj"], ln_g3, ln_b3, w_spatial, b_col, l, sbw, sgw, dsg)
        gs["sgu_ln_g"][l], gs["sgu_ln_b"][l] = dlng.reshape(sgw), dlnb.reshape(sgw)
        gs["w_spatial"][l], gs["b_spatial"][l] = dwsp, dbsp[:, :, 0]

        dproj = jnp.concatenate([dq, dk, dvb, du, dvs, dga, dgb], axis=1)
        gw_in = _mm_bwd_w("proj_in_dw", sv["h"], dproj, True)[0]
        late = (l, gw_in) if l > 0 else None
        dh, *parts = _mm_bwd_x("proj_in_dx", dproj, full["w_in"][l], tile_out(d, F32),
                               jobs=[scatter(gw_in, True, (0, half_d))] + ([] if late else [scatter(gw_in, True, (half_d, half_d))]))
        update("w_in", l, parts[0])
        if not late:
            update("w_in", l, parts[1], half_d)
        dx, dxb, dg = _rms_bwd("rms_mix_bwd", sv["x"], g_mix3, l, dh, dx1)
        gs["g_mix"][l] = dg.reshape(d)

    small_parts = _all_gather("gather_small", _pack([jnp.stack(gs[n]) for n in SMALL]))
    packed = _adam_small(small_parts, _pack([w[n] for n in SMALL]), _pack([mom[n] for n in SMALL]),
                         _pack([var[n] for n in SMALL]))
    unpacked = [_unpack(p, [w[n] for n in SMALL]) for p in packed]
    for i, n in enumerate(SMALL):
        res[n] = tuple(u[i] for u in unpacked)

    outs = [loss, dx[None]]
    for which in range(4):
        outs += [res[n][which] for n in ORDER]
    return tuple(outs)
```

```python
import functools

import jax
import jax.numpy as jnp
from jax import lax
from jax.experimental import pallas as pl
from jax.experimental.pallas import tpu as pltpu

F32 = jnp.float32
BF16 = jnp.bfloat16
SDS = jax.ShapeDtypeStruct
MESH = pl.DeviceIdType.MESH

N_DEV = 8
HEAD_DIM = 128
GROUP_DIM = 128
SGU_LEN = 128
SGU_CAUSAL = 64
NORM_EPS = 1e-6
ATT_BLOCK = 128
ATT_DEAD = -110.0
ATT_CHAINS = 4
ATT_CHAINS_FWD = 8

ADAM_LR, ADAM_B1, ADAM_B2, ADAM_EPS, ADAM_WD, ADAM_STEP = 0.001, 0.9, 0.999, 1e-08, 0.01, 10

ROW_TILE = 512
MM_TM, MM_TN, MM_TK = 1024, 1536, 2304


def _tile(n, target, align=128):
    best = None
    for t in range(align, min(n, target) + 1, align):
        if n % t == 0:
            best = t
    return n if best is None else best


class _Exchange:
    def __init__(self, src, kind, rows=None, into=None):
        self.src, self.kind, self.rows, self.into = src, kind, rows, into
        n = rows[1] if rows else src.shape[-2]
        width = src.shape[-1]
        shape = {"rows": (N_DEV,) + src.shape, "cols": (src.shape[0], N_DEV * width),
                 "blocks": (N_DEV, n, width), "slabs": (N_DEV, n, width // N_DEV)}[kind]
        self.out = SDS(shape, src.dtype)
        self.sems = [pltpu.SemaphoreType.DMA((N_DEV - 1,)), pltpu.SemaphoreType.DMA((N_DEV - 1,)),
                     pltpu.SemaphoreType.DMA(())]

    def plan(self, src_ref, out_ref, send_sems, recv_sems, local_sem):
        x, y, c = _position()
        me = 4 * x + 2 * y + c
        part = pl.ds(*self.rows) if self.rows else slice(None)
        width = self.out.shape[-1] // N_DEV if self.kind == "cols" else self.out.shape[-1]
        slab = lambda q: pl.ds(q * width, width)
        if self.kind == "rows":
            src_of, dst_of = (lambda p: src_ref.at[part]), (lambda q: out_ref.at[q, part])
        elif self.kind == "cols":
            src_of, dst_of = (lambda p: src_ref.at[part]), (lambda q: out_ref.at[part, slab(q)])
        elif self.kind == "blocks":
            src_of, dst_of = (lambda p: src_ref.at[p, part]), (lambda q: out_ref.at[q])
        else:
            src_of, dst_of = (lambda p: src_ref.at[part, slab(p)]), (lambda q: out_ref.at[q])
        local = pltpu.make_async_copy(src_of(me), dst_of(me), local_sem)
        sends, arrivals = [], []
        for k in range(1, N_DEV):
            px = 1 - x if k & 4 else x
            py = 1 - y if k & 2 else y
            pc = 1 - c if k & 1 else c
            peer = 4 * px + 2 * py + pc
            for dst, keep in ((dst_of(me), sends), (dst_of(peer), arrivals)):
                keep.append(pltpu.make_async_remote_copy(
                    src_ref=src_of(peer), dst_ref=dst, send_sem=send_sems.at[k - 1], recv_sem=recv_sems.at[k - 1],
                    device_id=(px, py, pc), device_id_type=MESH))
        return local, sends, arrivals


def _call(name, grid, ins, outs, body, scratch=(), semantics=None, jobs=(), aliases=None):
    n_in, n_out, n_scr, n_job = len(ins), len(outs), len(scratch), len(jobs)
    any_spec = pl.BlockSpec(memory_space=pl.ANY)
    aliases = dict(aliases or {})
    completed = [(n, jb.into) for n, jb in enumerate(jobs) if jb.into is not None]
    for pos, (n, _) in enumerate(completed):
        aliases[n_in + n_job + pos] = n_out + n

    def wrapped(*refs):
        own_in, job_in = refs[:n_in], refs[n_in:n_in + n_job]
        p = n_in + n_job + len(completed)
        own_out, job_out = refs[p:p + n_out], refs[p + n_out:p + n_out + n_job]
        p += n_out + n_job
        own_scr, job_sems = refs[p:p + n_scr], refs[p + n_scr:]
        if not jobs:
            return body(*own_in, *own_out, *own_scr)
        ids = [pl.program_id(a) for a in range(len(grid))]
        first = functools.reduce(jnp.logical_and, [i == 0 for i in ids])
        last = functools.reduce(jnp.logical_and, [i == g - 1 for i, g in zip(ids, grid)])
        plans = [jb.plan(job_in[n], job_out[n], *job_sems[3 * n:3 * n + 3]) for n, jb in enumerate(jobs)]

        @pl.when(first)
        def _():
            for local, sends, _ in plans:
                local.start()
                for cp in sends:
                    cp.start()

        body(*own_in, *own_out, *own_scr)

        @pl.when(last)
        def _():
            for local, sends, arrivals in plans:
                for cp in arrivals:
                    cp.wait_recv()
                for cp in sends:
                    cp.wait_send()
                local.wait()

    spec = lambda blk, im: any_spec if blk is None else pl.BlockSpec(blk, im)
    res = pl.pallas_call(
        wrapped, name=name, grid=grid,
        in_specs=[spec(blk, im) for _, blk, im in ins] + [any_spec] * (n_job + len(completed)),
        out_specs=[spec(blk, im) for _, blk, im in outs] + [any_spec] * n_job,
        out_shape=[s for s, _, _ in outs] + [jb.out for jb in jobs],
        scratch_shapes=list(scratch) + [sem for jb in jobs for sem in jb.sems],
        input_output_aliases=aliases,
        compiler_params=pltpu.CompilerParams(
            dimension_semantics=("arbitrary",) * len(grid) if jobs or not semantics else semantics),
    )(*[a for a, _, _ in ins], *[jb.src for jb in jobs], *[into for _, into in completed])
    return res


def _dot(a, b, dims=((1,), (0,))):
    return lax.dot_general(a, b, (dims, ((), ())), preferred_element_type=F32)


NN = ((1,), (0,))
NT = ((1,), (1,))
TN = ((0,), (0,))


def _matmul(name, grid, a, b, dims, outs, epilogue=None, extras=(), jobs=()):
    nk = grid[2]
    n_ex, n_out = len(extras), len(outs)
    acc_shape = tuple(d for d in outs[0][1] if d is not None)

    def body(*refs):
        a_ref, b_ref = refs[0], refs[1]
        ex_refs = refs[2:2 + n_ex]
        out_refs = refs[2 + n_ex:2 + n_ex + n_out]
        def product():
            return _dot(a_ref[...].astype(BF16), b_ref[...].astype(BF16), dims)

        def finish(acc):
            vals = (acc,) if epilogue is None else epilogue(acc, *[r[...] for r in ex_refs])
            for r, v in zip(out_refs, vals):
                r[...] = v.astype(r.dtype)

        if nk == 1:
            finish(product())
        else:
            acc_ref = refs[-1]
            k = pl.program_id(2)

            @pl.when(k == 0)
            def _():
                acc_ref[...] = jnp.zeros_like(acc_ref)

            @pl.when(k < nk - 1)
            def _():
                acc_ref[...] += product()

            @pl.when(k == nk - 1)
            def _():
                finish(acc_ref[...] + product())

    return _call(name, grid, [a, b, *extras], list(outs), body,
                 scratch=[pltpu.VMEM(acc_shape, F32)] if nk > 1 else [],
                 semantics=("parallel", "parallel", "arbitrary"), jobs=jobs)


def _whole(gathered, col_sharded):
    if col_sharded:
        return jnp.transpose(gathered, (1, 0, 2)).reshape(gathered.shape[1], -1)
    return gathered.reshape(-1, gathered.shape[2])


def _n_tile(n, target):
    return _tile(n, MM_TN if n % MM_TN == 0 else target)


def _mm_fwd(name, a, w, outs_fn, epilogue=None, extras_fn=None, tn=1024, jobs=()):
    m, k_dim = a.shape
    n_out = w.shape[1]
    tm, tn, tk = _tile(m, MM_TM, 8), _n_tile(n_out, tn), _tile(k_dim, MM_TK)
    grid = (m // tm, n_out // tn, k_dim // tk)
    extras = extras_fn(tm, tn) if extras_fn else ()
    return _matmul(name, grid, (a, (tm, tk), lambda i, j, k: (i, k)), (w, (tk, tn), lambda i, j, k: (k, j)), NN,
                   outs_fn(tm, tn), epilogue, extras, jobs)


def _mm_bwd_x(name, a, w, outs_fn, epilogue=None, extras_fn=None, tn=1024, jobs=()):
    m, k_dim = a.shape
    n_out = w.shape[0]
    tm, tn, tk = _tile(m, MM_TM, 8), _tile(n_out, tn), _tile(k_dim, MM_TK)
    grid = (m // tm, n_out // tn, k_dim // tk)
    extras = extras_fn(tm, tn) if extras_fn else ()
    return _matmul(name, grid, (a, (tm, tk), lambda i, j, k: (i, k)), (w, (tn, tk), lambda i, j, k: (j, k)), NT,
                   outs_fn(tm, tn), epilogue, extras, jobs)


def _mm_bwd_w(name, a, b, col_sharded, jobs=()):
    m, ka = a.shape
    n = b.shape[1]
    ts = _tile(m, 2 * MM_TM, 8)
    tj = _tile(n, 1024)
    if col_sharded:
        ti = _tile(ka, 1024)
        out = (SDS((ka, n), BF16), (ti, tj), lambda i, j, k: (i, j))
    else:
        shard = ka // N_DEV
        ti = _tile(shard, 1024)
        ips = shard // ti
        out = (SDS((N_DEV, shard, n), BF16), (None, ti, tj), lambda i, j, k: (i // ips, i % ips, j))
    grad, *got = _matmul(name, (ka // ti, n // tj, m // ts), (a, (ts, ti), lambda i, j, k: (k, i)),
                         (b, (ts, tj), lambda i, j, k: (k, j)), TN, [out], jobs=jobs)
    return grad, got


def _rms_fwd(name, x, g3, l):
    s, d = x.shape
    ts = _tile(s, ROW_TILE, 8)

    def body(x_ref, g_ref, h_ref):
        xv = x_ref[...]
        r = lax.rsqrt(jnp.mean(xv * xv, axis=-1, keepdims=True) + NORM_EPS)
        h_ref[...] = (xv * r * g_ref[...]).astype(BF16)

    return _call(name, (s // ts,), [(x, (ts, d), lambda i: (i, 0)), (g3, (None, 1, d), lambda i: (l, 0, 0))],
                 [(SDS((s, d), BF16), (ts, d), lambda i: (i, 0))], body, semantics=("parallel",))[0]


def _rms_bwd(name, x, g3, l, dh, dres):
    s, d = x.shape
    ts = _tile(s, ROW_TILE, 8)

    def body(x_ref, g_ref, dh_ref, dres_ref, dx_ref, dxb_ref, dg_ref):
        xv = x_ref[...]
        r = lax.rsqrt(jnp.mean(xv * xv, axis=-1, keepdims=True) + NORM_EPS)
        xhat = xv * r
        dhv = dh_ref[...]
        gy = dhv * g_ref[...]
        m = jnp.mean(gy * xhat, axis=-1, keepdims=True)
        dx = dres_ref[...] + r * (gy - xhat * m)
        dx_ref[...] = dx
        dxb_ref[...] = dx.astype(BF16)
        part = jnp.sum(dhv * xhat, axis=0, keepdims=True)

        @pl.when(pl.program_id(0) == 0)
        def _():
            dg_ref[...] = part

        @pl.when(pl.program_id(0) > 0)
        def _():
            dg_ref[...] += part

    row = lambda i: (i, 0)
    return _call(name, (s // ts,),
                 [(x, (ts, d), row), (g3, (None, 1, d), lambda i: (l, 0, 0)), (dh, (ts, d), row), (dres, (ts, d), row)],
                 [(SDS((s, d), F32), (ts, d), row), (SDS((s, d), BF16), (ts, d), row),
                  (SDS((1, d), F32), (1, d), lambda i: (0, 0))], body)


def _loss_head(y, target):
    s, d = y.shape
    ts = _tile(s, ROW_TILE, 8)

    def body(y_ref, t_ref, dy_ref, dyb_ref, sq_ref):
        diff = y_ref[...] - t_ref[...]
        dy = diff / d
        dy_ref[...] = dy
        dyb_ref[...] = dy.astype(BF16)
        part = jnp.sum(diff * diff, axis=0, keepdims=True)

        @pl.when(pl.program_id(0) == 0)
        def _():
            sq_ref[...] = part

        @pl.when(pl.program_id(0) > 0)
        def _():
            sq_ref[...] += part

    row = lambda i: (i, 0)
    return _call("loss_head", (s // ts,), [(y, (ts, d), row), (target, (ts, d), row)],
                 [(SDS((s, d), F32), (ts, d), row), (SDS((s, d), BF16), (ts, d), row),
                  (SDS((1, d), F32), (1, d), lambda i: (0, 0))], body)


def _qkv_prep(proj, gq3, gk3, l, sbw):
    s = proj.shape[0]
    ts = _tile(s, ROW_TILE, 8)
    heads = sbw // HEAD_DIM

    def body(q_ref, k_ref, v_ref, gq_ref, gk_ref, qn_ref, kn_ref, vb_ref):
        for h in range(heads):
            sl = slice(h * HEAD_DIM, (h + 1) * HEAD_DIM)
            for src, g_ref, dst in ((q_ref, gq_ref, qn_ref), (k_ref, gk_ref, kn_ref)):
                t = src[:, sl]
                r = lax.rsqrt(jnp.mean(t * t, axis=-1, keepdims=True) + NORM_EPS)
                dst[:, sl] = (t * r * g_ref[:, sl]).astype(BF16)
        vb_ref[...] = v_ref[...].astype(BF16)

    gspec = lambda g: (g, (None, 1, sbw), lambda i: (l, 0, 0))
    col = lambda c: (proj, (ts, sbw), lambda i: (i, c))
    out = (SDS((s, sbw), BF16), (ts, sbw), lambda i: (i, 0))
    return _call("qkv_prep", (s // ts,), [col(0), col(1), col(2), gspec(gq3), gspec(gk3)], [out, out, out], body,
                 semantics=("parallel",))


def _qk_bwd(proj, gq3, gk3, l, sbw, dqn, dkn, dv):
    s = proj.shape[0]
    ts = _tile(s, ROW_TILE, 8)
    heads = sbw // HEAD_DIM

    def body(q_ref, k_ref, gq_ref, gk_ref, dqn_ref, dkn_ref, dv_ref, dq_ref, dk_ref, dvb_ref, dgq_ref, dgk_ref):
        first = pl.program_id(0) == 0
        for src, g_ref, dn_ref, dst, dg_ref in ((q_ref, gq_ref, dqn_ref, dq_ref, dgq_ref),
                                                (k_ref, gk_ref, dkn_ref, dk_ref, dgk_ref)):
            for h in range(heads):
                sl = slice(h * HEAD_DIM, (h + 1) * HEAD_DIM)
                t = src[:, sl]
                r = lax.rsqrt(jnp.mean(t * t, axis=-1, keepdims=True) + NORM_EPS)
                xhat = t * r
                dn = dn_ref[:, sl]
                gy = dn * g_ref[:, sl]
                m = jnp.mean(gy * xhat, axis=-1, keepdims=True)
                dst[:, sl] = (r * (gy - xhat * m)).astype(BF16)
                part = jnp.sum(dn * xhat, axis=0, keepdims=True)

                @pl.when(first)
                def _():
                    dg_ref[:, sl] = part

                @pl.when(jnp.logical_not(first))
                def _():
                    dg_ref[:, sl] += part
        dvb_ref[...] = dv_ref[...].astype(BF16)

    gspec = lambda g: (g, (None, 1, sbw), lambda i: (l, 0, 0))
    col = lambda c: (proj, (ts, sbw), lambda i: (i, c))
    row = lambda a: (a, (ts, sbw), lambda i: (i, 0))
    outb = (SDS((s, sbw), BF16), (ts, sbw), lambda i: (i, 0))
    outg = (SDS((1, sbw), F32), (1, sbw), lambda i: (0, 0))
    return _call("qk_bwd", (s // ts,), [col(0), col(1), gspec(gq3), gspec(gk3), row(dqn), row(dkn), row(dv)],
                 [outb, outb, outb, outg, outg], body)


def _softplus(z):
    return jnp.maximum(z, 0.0) + jnp.log1p(jnp.exp(-jnp.abs(z)))


def _tri_dot(x, tri):
    hi = x.astype(BF16)
    lo = (x - hi.astype(F32)).astype(BF16)
    return _dot(hi, tri) + _dot(lo, tri)


def _att_common(tb):
    rows = lax.broadcasted_iota(jnp.int32, (tb, tb), 0)
    cols = lax.broadcasted_iota(jnp.int32, (tb, tb), 1)
    suffix = (rows >= cols).astype(BF16)
    prefix = (rows <= cols).astype(BF16)
    return rows, cols, suffix, prefix, jnp.ones((tb, tb), BF16)


def _key_norm_max(k_ref, kmax_ref):
    kf = k_ref[...].astype(F32)
    n2 = jnp.sum(kf * kf, axis=1, keepdims=True)
    kmax_ref[...] = jnp.broadcast_to(jnp.max(n2, axis=0, keepdims=True), kmax_ref.shape)


def _att_reach(q, kmax_ref, scale):
    qf = q.astype(F32)
    qn2 = jnp.sum(qf * qf, axis=1, keepdims=True)
    return scale * jnp.sqrt(qn2 * kmax_ref[0:1, 0:1]) * 1.001 + 1e-3


def _att_chains(s, tb, want=ATT_CHAINS):
    return want if s % (want * tb) == 0 else 1


def _att_block(diag, j, tb, rows, cols):
    kb = diag - j
    off = pl.multiple_of(jnp.maximum(kb, 0) * tb, tb)
    limit = jnp.where(kb >= 0, diag * tb, -(1 << 30))
    return off, (cols + kb * tb) < (rows + limit)


def _att_blocks(diag, j, tb, rows, cols, masked):
    if masked:
        return zip(*[_att_block(dg, j, tb, rows, cols) for dg in diag])
    return [pl.multiple_of((dg - j) * tb, tb) for dg in diag], None


def _att_sweeps(sweep, diag):
    alive = lambda last: lambda carry: jnp.logical_and(carry[0] <= last, carry[1])
    carry = sweep(True)((jnp.int32(0), jnp.bool_(True)))
    carry = lax.while_loop(alive(diag[0]), sweep(False), carry)
    return lax.while_loop(alive(diag[-1]), sweep(True), carry)[0]


def _attn_fwd(qn, kn, vb, jobs=()):
    s, sbw = qn.shape
    heads = sbw // HEAD_DIM
    tb = _tile(s, ATT_BLOCK, 8)
    nch = _att_chains(s, tb, ATT_CHAINS_FWD)
    tq = nch * tb
    scale = HEAD_DIM ** -0.5

    def body(q_ref, k_ref, v_ref, o_ref, kmax_ref, reach_ref, run_ref, acc_ref):
        qi = pl.program_id(1)

        @pl.when(qi == 0)
        def _():
            _key_norm_max(k_ref, kmax_ref)

        rows, cols, suffix, _, ones = _att_common(tb)
        tri = jnp.concatenate([suffix, ones], axis=1)
        diag = [qi * nch + c for c in range(nch)]
        for c in range(nch):
            q = q_ref[c * tb:(c + 1) * tb, :]
            reach_ref[c] = jnp.broadcast_to(_att_reach(q, kmax_ref, scale), (tb, tb))
        run_ref[...] = jnp.zeros_like(run_ref)
        acc_ref[...] = jnp.zeros_like(acc_ref)

        def sweep(masked):
            def step(carry):
                j, alive, chains = carry[0], None, range(nch)
                offs, pasts = _att_blocks(diag, j, tb, rows, cols, masked)
                hide = (lambda c, val: jnp.where(pasts[c], val, 0.0)) if masked else (lambda c, val: val)
                runs = [run_ref[c] for c in chains]
                zs = [_dot(q_ref[c * tb:(c + 1) * tb, :], k_ref[pl.ds(offs[c], tb), :], NT) * scale for c in chains]
                keeps = [hide(c, -_softplus(zs[c])) for c in chains]
                sums = [_tri_dot(keeps[c], tri) for c in chains]
                probs = [hide(c, jnp.exp(zs[c] + sums[c][:, :tb] + runs[c])).astype(BF16) for c in chains]
                outs = [_dot(probs[c], v_ref[pl.ds(offs[c], tb), :]) for c in chains]
                for c in chains:
                    acc_ref[c] += outs[c]
                    run = runs[c] + sums[c][:, tb:]
                    run_ref[c] = run
                    m = jnp.max(run + reach_ref[c])
                    alive = m if alive is None else jnp.maximum(alive, m)
                return j + 1, alive > ATT_DEAD
            return step

        _att_sweeps(sweep, diag)
        for c in range(nch):
            o_ref[c * tb:(c + 1) * tb, :] = acc_ref[c].astype(o_ref.dtype)

    qspec = lambda a: (a, (tq, HEAD_DIM), lambda h, i: (i, h))
    full = lambda a: (a, (s, HEAD_DIM), lambda h, i: (0, h))
    state = pltpu.VMEM((nch, tb, tb), F32)
    return _call("attn_fwd", (heads, s // tq), [qspec(qn), full(kn), full(vb)],
                 [(SDS((s, sbw), BF16), (tq, HEAD_DIM), lambda h, i: (i, h))], body,
                 scratch=[pltpu.VMEM((8, 128), F32), state, state, pltpu.VMEM((nch, tb, HEAD_DIM), F32)],
                 semantics=("parallel", "arbitrary"), jobs=jobs)


def _attn_bwd(qn, kn, vb, do, jobs=()):
    s, sbw = qn.shape
    heads = sbw // HEAD_DIM
    tb = _tile(s, ATT_BLOCK, 8)
    nch = _att_chains(s, tb)
    tq = nch * tb
    scale = HEAD_DIM ** -0.5

    def body(q_ref, k_ref, v_ref, do_ref, dq_ref, dk_ref, dv_ref, kmax_ref, reach_ref, run_ref, g_ref):
        qi = pl.program_id(1)

        @pl.when(qi == 0)
        def _():
            _key_norm_max(k_ref, kmax_ref)
            dk_ref[...] = jnp.zeros_like(dk_ref)
            dv_ref[...] = jnp.zeros_like(dv_ref)

        rows, cols, suffix, prefix, ones = _att_common(tb)
        tri_back = jnp.concatenate([suffix, ones], axis=1)
        tri_fwd = jnp.concatenate([prefix, ones], axis=1)
        diag = [qi * nch + c for c in range(nch)]
        rows_of = lambda ref, c: ref[c * tb:(c + 1) * tb, :]
        for c in range(nch):
            reach_ref[c] = jnp.broadcast_to(_att_reach(rows_of(q_ref, c), kmax_ref, scale), (tb, tb))
        run_ref[...] = jnp.zeros_like(run_ref)
        dq_ref[...] = jnp.zeros_like(dq_ref)

        chains = range(nch)

        def sweep_back(masked):
            def step(carry):
                j, alive = carry[0], None
                offs, pasts = _att_blocks(diag, j, tb, rows, cols, masked)
                hide = (lambda c, val: jnp.where(pasts[c], val, 0.0)) if masked else (lambda c, val: val)
                runs = [run_ref[c] for c in chains]
                zs = [_dot(rows_of(q_ref, c), k_ref[pl.ds(offs[c], tb), :], NT) * scale for c in chains]
                das = [_dot(rows_of(do_ref, c), v_ref[pl.ds(offs[c], tb), :], NT) for c in chains]
                keeps = [hide(c, -_softplus(zs[c])) for c in chains]
                sums = [_tri_dot(keeps[c], tri_back) for c in chains]
                probs = [hide(c, jnp.exp(zs[c] + sums[c][:, :tb] + runs[c])) for c in chains]
                dvs = [_dot(probs[c].astype(BF16), rows_of(do_ref, c), TN) for c in chains]
                for c in chains:
                    g_ref[c, j] = probs[c] * das[c]
                    dv_ref[pl.ds(offs[c], tb), :] += dvs[c]
                    run = runs[c] + sums[c][:, tb:]
                    run_ref[c] = run
                    m = jnp.max(run + reach_ref[c])
                    alive = m if alive is None else jnp.maximum(alive, m)
                return j + 1, alive > ATT_DEAD
            return step

        nsteps = _att_sweeps(sweep_back, diag)
        run_ref[...] = jnp.zeros_like(run_ref)

        def sweep_fwd(i, carry):
            j = nsteps - 1 - i
            offs, pasts = zip(*[_att_block(diag[c], j, tb, rows, cols) for c in chains])
            runs = [run_ref[c] for c in chains]
            ks = [k_ref[pl.ds(offs[c], tb), :] for c in chains]
            sgs = [jax.nn.sigmoid(_dot(rows_of(q_ref, c), ks[c], NT) * scale) for c in chains]
            gs = [g_ref[c, j] for c in chains]
            sums = [_tri_dot(gs[c], tri_fwd) for c in chains]
            dzs = [(jnp.where(pasts[c], gs[c] - sgs[c] * (runs[c] + sums[c][:, :tb]), 0.0) * scale).astype(BF16)
                   for c in chains]
            dqs = [_dot(dzs[c], ks[c]) for c in chains]
            dks = [_dot(dzs[c], rows_of(q_ref, c), TN) for c in chains]
            for c in chains:
                dq_ref[c * tb:(c + 1) * tb, :] += dqs[c]
                dk_ref[pl.ds(offs[c], tb), :] += dks[c]
                run_ref[c] = runs[c] + sums[c][:, tb:]
            return carry

        lax.fori_loop(0, nsteps, sweep_fwd, 0)

    qspec = lambda a: (a, (tq, HEAD_DIM), lambda h, i: (i, h))
    full = lambda a: (a, (s, HEAD_DIM), lambda h, i: (0, h))
    outq = (SDS((s, sbw), F32), (tq, HEAD_DIM), lambda h, i: (i, h))
    outf = (SDS((s, sbw), F32), (s, HEAD_DIM), lambda h, i: (0, h))
    state = pltpu.VMEM((nch, tb, tb), F32)
    return _call("attn_bwd", (heads, s // tq), [qspec(qn), full(kn), full(vb), qspec(do)], [outq, outf, outf], body,
                 scratch=[pltpu.VMEM((8, 128), F32), state, state, pltpu.VMEM((nch, s // tb, tb, tb), F32)],
                 semantics=("parallel", "arbitrary"), jobs=jobs)


_SQRT_HALF = 0.7071067811865476
_INV_SQRT_2PI = 0.3989422804014327


def _gelu(x):
    return 0.5 * x * (1.0 + lax.erf(x * _SQRT_HALF))


def _gelu_grad(x):
    return 0.5 * (1.0 + lax.erf(x * _SQRT_HALF)) + x * jnp.exp(-0.5 * x * x) * _INV_SQRT_2PI


def _sgu_mask():
    i = lax.broadcasted_iota(jnp.int32, (SGU_LEN, SGU_LEN), 0) // SGU_CAUSAL
    j = lax.broadcasted_iota(jnp.int32, (SGU_LEN, SGU_LEN), 1) // SGU_CAUSAL
    return j <= i


def _sgu_norm(vg, lng, lnb):
    mu = jnp.mean(vg, axis=-1, keepdims=True)
    xc = vg - mu
    rstd = lax.rsqrt(jnp.mean(xc * xc, axis=-1, keepdims=True) + NORM_EPS)
    xhat = xc * rstd
    return xhat, rstd, xhat * lng + lnb


def _sgu_specs(proj, ln_g3, ln_b3, w_sp, b_col, l, sbw, sgw, rb):
    groups = sgw // GROUP_DIM
    ucol, vcol = 3 * sbw // sgw, 3 * sbw // sgw + 1
    vec = lambda a: (a, (None, 1, sgw), lambda i: (l, 0, 0))
    mat = lambda a: (a, (None, groups, SGU_LEN, SGU_LEN), lambda i: (l, 0, 0, 0))
    return [(proj, (rb, sgw), lambda i: (i, ucol)), (proj, (rb, sgw), lambda i: (i, vcol)),
            vec(ln_g3), vec(ln_b3), mat(w_sp), mat(b_col)]


def _sgu_fwd(proj, ln_g3, ln_b3, w_sp, b_col, l, sbw, sgw):
    s = proj.shape[0]
    rb = _tile(s, 2 * SGU_LEN, SGU_LEN)
    groups = sgw // GROUP_DIM

    def body(u_ref, v_ref, lng_ref, lnb_ref, w_ref, b_ref, s_ref):
        mask = _sgu_mask()
        _, _, vln = _sgu_norm(_gelu(v_ref[...]), lng_ref[...], lnb_ref[...])
        vb = vln.astype(BF16)
        for g in range(groups):
            cs = slice(g * GROUP_DIM, (g + 1) * GROUP_DIM)
            wm = jnp.where(mask, w_ref[g], 0.0).astype(BF16)
            for c in range(rb // SGU_LEN):
                rs = slice(c * SGU_LEN, (c + 1) * SGU_LEN)
                mixed = _dot(wm, vb[rs, cs]) + b_ref[g]
                s_ref[rs, cs] = (_gelu(u_ref[rs, cs]) * mixed).astype(BF16)

    return _call("sgu_fwd", (s // rb,), _sgu_specs(proj, ln_g3, ln_b3, w_sp, b_col, l, sbw, sgw, rb),
                 [(SDS((s, sgw), BF16), (rb, sgw), lambda i: (i, 0))], body, semantics=("parallel",))[0]


def _sgu_bwd(proj, ln_g3, ln_b3, w_sp, b_col, l, sbw, sgw, ds):
    s = proj.shape[0]
    rb = _tile(s, 2 * SGU_LEN, SGU_LEN)
    groups = sgw // GROUP_DIM
    nsteps = s // rb

    def body(u_ref, v_ref, lng_ref, lnb_ref, w_ref, b_ref, ds_ref,
             du_ref, dv_ref, dlng_ref, dlnb_ref, dw_ref, db_ref, dvln_ref, dw_acc, db_acc):
        step = pl.program_id(0)

        @pl.when(step == 0)
        def _():
            dw_acc[...] = jnp.zeros_like(dw_acc)
            db_acc[...] = jnp.zeros_like(db_acc)

        mask = _sgu_mask()
        vp = v_ref[...]
        lng = lng_ref[...]
        xhat, rstd, vln = _sgu_norm(_gelu(vp), lng, lnb_ref[...])
        vb = vln.astype(BF16)
        for g in range(groups):
            cs = slice(g * GROUP_DIM, (g + 1) * GROUP_DIM)
            wm = jnp.where(mask, w_ref[g], 0.0).astype(BF16)
            for c in range(rb // SGU_LEN):
                rs = slice(c * SGU_LEN, (c + 1) * SGU_LEN)
                vbp = vb[rs, cs]
                mixed = _dot(wm, vbp) + b_ref[g]
                up = u_ref[rs, cs]
                dsp = ds_ref[rs, cs]
                du_ref[rs, cs] = (dsp * mixed * _gelu_grad(up)).astype(BF16)
                dmixed = dsp * _gelu(up)
                dmb = dmixed.astype(BF16)
                dvln_ref[rs, cs] = _dot(wm, dmb, TN)
                dw_acc[g] += _dot(dmb, vbp, NT)
                db_acc[g] += dmixed
        dvln = dvln_ref[...]
        part_g = jnp.sum(dvln * xhat, axis=0, keepdims=True)
        part_b = jnp.sum(dvln, axis=0, keepdims=True)

        @pl.when(step == 0)
        def _():
            dlng_ref[...] = part_g
            dlnb_ref[...] = part_b

        @pl.when(step > 0)
        def _():
            dlng_ref[...] += part_g
            dlnb_ref[...] += part_b

        dxhat = dvln * lng
        m1 = jnp.mean(dxhat, axis=-1, keepdims=True)
        m2 = jnp.mean(dxhat * xhat, axis=-1, keepdims=True)
        dv_ref[...] = (rstd * (dxhat - m1 - xhat * m2) * _gelu_grad(vp)).astype(BF16)

        @pl.when(step == nsteps - 1)
        def _():
            for g in range(groups):
                dw_ref[g] = jnp.where(mask, dw_acc[g], 0.0)
                db_ref[g] = jnp.broadcast_to(jnp.sum(db_acc[g], axis=-1, keepdims=True), (SGU_LEN, SGU_LEN))

    row = lambda i: (i, 0)
    outb = (SDS((s, sgw), BF16), (rb, sgw), row)
    outv = (SDS((1, sgw), F32), (1, sgw), lambda i: (0, 0))
    outm = (SDS((groups, SGU_LEN, SGU_LEN), F32), (groups, SGU_LEN, SGU_LEN), lambda i: (0, 0, 0))
    acc = pltpu.VMEM((groups, SGU_LEN, SGU_LEN), F32)
    return _call("sgu_bwd", (nsteps,),
                 _sgu_specs(proj, ln_g3, ln_b3, w_sp, b_col, l, sbw, sgw, rb) + [(ds, (rb, sgw), row)],
                 [outb, outb, outv, outv, outm, outm], body, scratch=[pltpu.VMEM((rb, sgw), F32), acc, acc])


def _merge_fwd(o, sg, w_oa, w_ob, proj, d, gate_off, jobs=()):
    s, kw = o.shape
    tm, tn = _tile(s, MM_TM, 8), _tile(d, 512)

    def body(o_ref, s_ref, wa_ref, wb_ref, ga_ref, gb_ref, ya_ref, yb_ref, mg_ref):
        ya = _dot(o_ref[...], wa_ref[...])
        yb = _dot(s_ref[...], wb_ref[...])
        ya_ref[...] = ya.astype(ya_ref.dtype)
        yb_ref[...] = yb.astype(yb_ref.dtype)
        mg_ref[...] = (jax.nn.sigmoid(ga_ref[...]) * ya + jax.nn.sigmoid(gb_ref[...]) * yb).astype(BF16)

    act = lambda a: (a, (tm, kw), lambda i, j: (i, 0))
    wsp = lambda w: (w, (kw, tn), lambda i, j: (0, j))
    gate = lambda off: (proj, (tm, tn), lambda i, j: (i, off // tn + j))
    tile = lambda dt: (SDS((s, d), dt), (tm, tn), lambda i, j: (i, j))
    return _call("merge_fwd", (s // tm, d // tn), [act(o), act(sg), wsp(w_oa), wsp(w_ob), gate(gate_off), gate(gate_off + d)],
                 [tile(BF16), tile(BF16), tile(BF16)], body, semantics=("parallel", "parallel"), jobs=jobs)


def _merge_bwd_epilogue(dm, ya, yb, ga, gb):
    sa, sb = jax.nn.sigmoid(ga), jax.nn.sigmoid(gb)
    return dm * ya * sa * (1.0 - sa), dm * yb * sb * (1.0 - sb), dm * sa, dm * sb


def _position():
    x, y, c = lax.axis_index("x"), lax.axis_index("y"), lax.axis_index("c")
    return x, y, c


def _all_gather(name, shard):
    def body(x_ref, out_ref, send_sems, recv_sems, local_sem):
        x, y, c = _position()
        me, sibling = (x, y, c), (x, y, 1 - c)
        chips = [(1 - x, y), (x, 1 - y), (1 - x, 1 - y)]

        def slot(px, py, pc):
            return out_ref.at[4 * px + 2 * py + pc]

        def copy(k, block, to, src=None):
            return pltpu.make_async_remote_copy(
                src_ref=slot(*block) if src is None else src, dst_ref=slot(*block),
                send_sem=send_sems.at[k], recv_sem=recv_sems.at[k], device_id=to, device_id_type=MESH)

        mine = pltpu.make_async_copy(x_ref, slot(*me), local_sem)
        mine.start()
        first = [copy(0, me, sibling, src=x_ref)]
        first += [copy(1 + j, me, (*chip, c), src=x_ref) for j, chip in enumerate(chips)]
        for cp in first:
            cp.start()
        passed = [copy(4 + j, (*chip, c), sibling) for j, chip in enumerate(chips)]
        for j, chip in enumerate(chips):
            copy(1 + j, (*chip, c), me).wait_recv()
            passed[j].start()
        copy(0, sibling, me).wait_recv()
        for j, chip in enumerate(chips):
            copy(4 + j, (*chip, 1 - c), me).wait_recv()
        for cp in first + passed:
            cp.wait_send()
        mine.wait()

    return pl.pallas_call(
        body, name=name, out_shape=SDS((N_DEV,) + shard.shape, shard.dtype),
        in_specs=[pl.BlockSpec(memory_space=pl.ANY)], out_specs=pl.BlockSpec(memory_space=pl.ANY),
        scratch_shapes=[pltpu.SemaphoreType.DMA((7,)), pltpu.SemaphoreType.DMA((7,)), pltpu.SemaphoreType.DMA(())],
    )(shard)


def _adam_math(g, w, m, v):
    m = ADAM_B1 * m + (1.0 - ADAM_B1) * g
    v = ADAM_B2 * v + (1.0 - ADAM_B2) * (g * g)
    m_hat = m / (1.0 - ADAM_B1 ** ADAM_STEP)
    v_hat = v / (1.0 - ADAM_B2 ** ADAM_STEP)
    delta = -ADAM_LR * (m_hat / (jnp.sqrt(v_hat) + ADAM_EPS) + ADAM_WD * w)
    return delta, m, v


def _adam(name, parts, w, m, v, l, prev, row0=0):
    _, r, c = parts.shape
    tr, tc = _tile(r, 256, 8), _tile(c, MM_TN)
    rb = row0 // tr

    def body(p_ref, w_ref, m_ref, v_ref, *rest):
        g_ref, d_ref, mo_ref, vo_ref = rest[-4:]
        g = p_ref[0].astype(F32)
        for q in range(1, N_DEV):
            g = g + p_ref[q].astype(F32)
        delta, mn, vn = _adam_math(g, w_ref[...], m_ref[...], v_ref[...])
        g_ref[...] = g
        d_ref[...] = delta
        mo_ref[...] = mn
        vo_ref[...] = vn

    cur = lambda a: (a, (None, tr, tc), lambda i, j: (l, rb + i, j))
    out = (SDS(w.shape, F32), (None, tr, tc), lambda i, j: (l, rb + i, j))
    ins = [(parts, (N_DEV, tr, tc), lambda i, j: (0, i, j)), cur(w), cur(m), cur(v)]
    aliases = None
    if prev is not None:
        ins += [(p, None, None) for p in prev]
        aliases = {4 + n: n for n in range(4)}
    return _call(name, (r // tr, c // tc), ins, [out, out, out, out], body, semantics=("parallel", "parallel"),
                 aliases=aliases)


def _adam_small(parts, w, m, v):
    r = w.shape[0]
    tr = _tile(r, 1096, 8)

    def body(p_ref, w_ref, m_ref, v_ref, g_ref, d_ref, mo_ref, vo_ref):
        g = p_ref[0]
        for q in range(1, N_DEV):
            g = g + p_ref[q]
        delta, mn, vn = _adam_math(g, w_ref[...], m_ref[...], v_ref[...])
        g_ref[...] = g
        d_ref[...] = delta
        mo_ref[...] = mn
        vo_ref[...] = vn

    cur = lambda a: (a, (tr, 128), lambda i: (i, 0))
    out = (SDS(w.shape, F32), (tr, 128), lambda i: (i, 0))
    return _call("adam_small", (r // tr,), [(parts, (N_DEV, tr, 128), lambda i: (0, i, 0)), cur(w), cur(m), cur(v)],
                 [out, out, out, out], body, semantics=("parallel",))


SMALL = ("g_mix", "g_q", "g_k", "sgu_ln_g", "sgu_ln_b", "w_spatial", "b_spatial", "g_ff")
BIG = ("w_in", "w_oa", "w_ob", "w_out", "w_ff1", "w_ff2")
COL_SHARDED = ("w_in", "w_oa", "w_ob", "w_ff1")
ORDER = ("g_mix", "w_in", "g_q", "g_k", "sgu_ln_g", "sgu_ln_b", "w_spatial", "b_spatial", "w_oa", "w_ob", "w_out",
         "g_ff", "w_ff1", "w_ff2")


def _pack(arrs):
    return jnp.concatenate([a.reshape(-1) for a in arrs]).reshape(-1, 128)


def _unpack(packed, like):
    flat, out, off = packed.reshape(-1), [], 0
    for a in like:
        out.append(flat[off:off + a.size].reshape(a.shape))
        off += a.size
    return out


def kernel(x, g_mix, w_in, g_q, g_k, sgu_ln_g, sgu_ln_b, w_spatial, b_spatial, w_oa, w_ob, w_out, g_ff, w_ff1, w_ff2, loss_target, m_g_mix, m_w_in, m_g_q, m_g_k, m_sgu_ln_g, m_sgu_ln_b, m_w_spatial, m_b_spatial, m_w_oa, m_w_ob, m_w_out, m_g_ff, m_w_ff1, m_w_ff2, v_g_mix, v_w_in, v_g_q, v_g_k, v_sgu_ln_g, v_sgu_ln_b, v_w_spatial, v_b_spatial, v_w_oa, v_w_ob, v_w_out, v_g_ff, v_w_ff1, v_w_ff2):
    w = dict(g_mix=g_mix, w_in=w_in, g_q=g_q, g_k=g_k, sgu_ln_g=sgu_ln_g, sgu_ln_b=sgu_ln_b, w_spatial=w_spatial,
             b_spatial=b_spatial, w_oa=w_oa, w_ob=w_ob, w_out=w_out, g_ff=g_ff, w_ff1=w_ff1, w_ff2=w_ff2)
    mom = dict(g_mix=m_g_mix, w_in=m_w_in, g_q=m_g_q, g_k=m_g_k, sgu_ln_g=m_sgu_ln_g, sgu_ln_b=m_sgu_ln_b,
               w_spatial=m_w_spatial, b_spatial=m_b_spatial, w_oa=m_w_oa, w_ob=m_w_ob, w_out=m_w_out, g_ff=m_g_ff,
               w_ff1=m_w_ff1, w_ff2=m_w_ff2)
    var = dict(g_mix=v_g_mix, w_in=v_w_in, g_q=v_g_q, g_k=v_g_k, sgu_ln_g=v_sgu_ln_g, sgu_ln_b=v_sgu_ln_b,
               w_spatial=v_w_spatial, b_spatial=v_b_spatial, w_oa=v_w_oa, w_ob=v_w_ob, w_out=v_w_out, g_ff=v_g_ff,
               w_ff1=v_w_ff1, w_ff2=v_w_ff2)

    xs = x[0]
    target = loss_target[0]
    s, d = xs.shape
    depth = g_mix.shape[0]
    sbw = g_q.shape[1] * g_q.shape[2]
    sgw = sgu_ln_g.shape[1]
    d_ff =w_ff1.shape[2] * N_DEV
    n_in = w_in.shape[2] * N_DEV
    gate_off = 3 * sbw + 2 * sgw

    wb = {n: w[n].astype(BF16) for n in BIG}
    full = {n: [None] * depth for n in BIG}

    for n in ("w_in", "w_oa", "w_ob", "w_out"):
        full[n][0] = _whole(_all_gather("gather_" + n, wb[n][0]), n in COL_SHARDED)

    def gather(names, l, rows=None, into=None):
        if l >= depth:
            return []
        return [_Exchange(wb[n][l], "cols" if n in COL_SHARDED else "rows", rows, into) for n in names]

    def arrived(names, l, gathered):
        for n, g in zip(names, gathered):
            full[n][l] = g if n in COL_SHARDED else g.reshape(-1, g.shape[2])

    def scatter(grad, col_sharded, rows=None):
        return _Exchange(grad, "slabs" if col_sharded else "blocks", rows)

    res = {}

    def update(n, l, parts, row0=0):
        res[n] = _adam("adam_" + n, parts, w[n], mom[n], var[n], l, res.get(n), row0)

    g_mix3, g_ff3 = g_mix.reshape(depth, 1, d), g_ff.reshape(depth, 1, d)
    gq3, gk3 = g_q.reshape(depth, 1, sbw), g_k.reshape(depth, 1, sbw)
    ln_g3, ln_b3 = sgu_ln_g.reshape(depth, 1, sgw), sgu_ln_b.reshape(depth, 1, sgw)
    b_col = jnp.broadcast_to(b_spatial[..., None], b_spatial.shape + (SGU_LEN,))

    def tile_out(n, dt):
        return lambda tm, tn: [(SDS((s, n), dt), (tm, tn), lambda i, j, k: (i, j))]

    def tile_in(a):
        return lambda tm, tn: [(a, (tm, tn), lambda i, j, k: (i, j))]

    def gate_tile(proj, off, tm, tn):
        return (proj, (tm, tn), lambda i, j, k: (i, off // tn + j))

    saved = []
    cur = xs
    for l in range(depth):
        h = _rms_fwd("rms_mix", cur, g_mix3, l)
        proj, *got = _mm_fwd("proj_in", h, full["w_in"][l], tile_out(n_in, F32), jobs=gather(["w_ff1"], l))
        arrived(["w_ff1"], l, got)
        qn, kn, vb = _qkv_prep(proj, gq3, gk3, l, sbw)
        o, *got = _attn_fwd(qn, kn, vb, jobs=gather(["w_ff2"], l))
        arrived(["w_ff2"], l, got)
        sg = _sgu_fwd(proj, ln_g3, ln_b3, w_spatial, b_col, l, sbw, sgw)
        ya, yb, merged, *got = _merge_fwd(o, sg, full["w_oa"][l], full["w_ob"][l], proj, d, gate_off,
                                          jobs=gather(["w_oa", "w_ob"], l + 1))
        arrived(["w_oa", "w_ob"], l + 1, got)
        x1, *got = _mm_fwd("proj_out", merged, full["w_out"][l], tile_out(d, F32), lambda acc, res: (res + acc,),
                           tile_in(cur), jobs=gather(["w_out"], l + 1))
        arrived(["w_out"], l + 1, got)
        h2 = _rms_fwd("rms_ff", x1, g_ff3, l)
        half = d // 2
        a1, r, *got = _mm_fwd("ff_up", h2, full["w_ff1"][l],
                              lambda tm, tn: tile_out(d_ff, F32)(tm, tn) + tile_out(d_ff, BF16)(tm, tn),
                              lambda acc: (acc, jnp.square(jnp.maximum(acc, 0.0))),
                              jobs=gather(["w_in"], l + 1, (0, half)))
        x2, *got = _mm_fwd("ff_down", r, full["w_ff2"][l], tile_out(d, F32), lambda acc, res: (res + acc,),
                           tile_in(x1), jobs=gather(["w_in"], l + 1, (half, half), got[0] if got else None))
        arrived(["w_in"], l + 1, got)
        saved.append(dict(x=cur, h=h, proj=proj, qn=qn, kn=kn, vb=vb, o=o, sg=sg, ya=ya, yb=yb, merged=merged,
                          x1=x1, h2=h2, a1=a1, r=r))
        cur = x2

    dx, dxb, sq = _loss_head(cur, target)
    loss = lax.psum(0.5 * jnp.sum(sq) / d, ("x", "y", "c"))

    gs = {n: [None] * depth for n in SMALL}
    late = None
    half_d, half_ff = d // 2, d_ff // N_DEV // 2
    for l in reversed(range(depth)):
        sv = saved[l]
        gw_ff2, got = _mm_bwd_w("ff_down_dw", sv["r"], dxb, False,
                                jobs=[scatter(late[1], True, (half_d, half_d))] if late else [])
        if late:
            update("w_in", late[0], got[0], half_d)
        da1, parts = _mm_bwd_x("ff_down_dx", dxb, full["w_ff2"][l], tile_out(d_ff, BF16),
                               lambda acc, a1: (acc * (2.0 * jnp.maximum(a1, 0.0)),), tile_in(sv["a1"]),
                               jobs=[scatter(gw_ff2, False, (0, half_ff))])
        update("w_ff2", l, parts)
        gw_ff1, got = _mm_bwd_w("ff_up_dw", sv["h2"], da1, True, jobs=[scatter(gw_ff2, False, (half_ff, half_ff))])
        update("w_ff2", l, got[0], half_ff)
        dh2, parts = _mm_bwd_x("ff_up_dx", da1, full["w_ff1"][l], tile_out(d, F32),
                               jobs=[scatter(gw_ff1, True, (0, half_d))])
        update("w_ff1", l, parts)
        dx1, dx1b, dg = _rms_bwd("rms_ff_bwd", sv["x1"], g_ff3, l, dh2, dx)
        gs["g_ff"][l] = dg.reshape(d)

        gw_out = _mm_bwd_w("proj_out_dw", sv["merged"], dx1b, False)[0]
        dga, dgb, dya, dyb = _mm_bwd_x(
            "proj_out_dx", dx1b, full["w_out"][l], lambda tm, tn: tile_out(d, BF16)(tm, tn) * 4, _merge_bwd_epilogue,
            lambda tm, tn: [(a, (tm, tn), lambda i, j, k: (i, j)) for a in (sv["ya"], sv["yb"])]
            + [gate_tile(sv["proj"], off, tm, tn) for off in (gate_off, gate_off + d)], tn=512)
        gw_oa = _mm_bwd_w("oa_dw", sv["o"], dya, True)[0]
        gw_ob = _mm_bwd_w("ob_dw", sv["sg"], dyb, True)[0]
        do = _mm_bwd_x("oa_dx", dya, full["w_oa"][l], tile_out(sbw, BF16))[0]
        dsg = _mm_bwd_x("ob_dx", dyb, full["w_ob"][l], tile_out(sgw, F32))[0]

        dqn, dkn, dv, *parts = _attn_bwd(sv["qn"], sv["kn"], sv["vb"], do,
                                         jobs=[scatter(gw_ff1, True, (half_d, half_d)), scatter(gw_out, False),
                                               scatter(gw_oa, True), scatter(gw_ob, True)])
        update("w_ff1", l, parts[0], half_d)
        for n, p in zip(("w_out", "w_oa", "w_ob"), parts[1:]):
            update(n, l, p)
        dq, dk, dvb, dgq, dgk = _qk_bwd(sv["proj"], gq3, gk3, l, sbw, dqn, dkn, dv)
        gs["g_q"][l], gs["g_k"][l] = dgq.reshape(g_q.shape[1:]), dgk.reshape(g_k.shape[1:])
        du, dvs, dlng, dlnb, dwsp, dbsp = _sgu_bwd(sv["proj"], ln_g3, ln_b3, w_spatial, b_col, l, sbw, sgw, dsg)
        gs["sgu_ln_g"][l], gs["sgu_ln_b"][l] = dlng.reshape(sgw), dlnb.reshape(sgw)
        gs["w_spatial"][l], gs["b_spatial"][l] = dwsp, dbsp[:, :, 0]

        dproj = jnp.concatenate([dq, dk, dvb, du, dvs, dga, dgb], axis=1)
        gw_in = _mm_bwd_w("proj_in_dw", sv["h"], dproj, True)[0]
        late = (l, gw_in) if l > 0 else None
        dh, *parts = _mm_bwd_x("proj_in_dx", dproj, full["w_in"][l], tile_out(d, F32),
                               jobs=[scatter(gw_in, True, (0, half_d))] + ([] if late else [scatter(gw_in, True, (half_d, half_d))]))
        update("w_in", l, parts[0])
        if not late:
            update("w_in", l, parts[1], half_d)
        dx, dxb, dg = _rms_bwd("rms_mix_bwd", sv["x"], g_mix3, l, dh, dx1)
        gs["g_mix"][l] = dg.reshape(d)

    small_parts = _all_gather("gather_small", _pack([jnp.stack(gs[n]) for n in SMALL]))
    packed = _adam_small(small_parts, _pack([w[n] for n in SMALL]), _pack([mom[n] for n in SMALL]),
                         _pack([var[n] for n in SMALL]))
    unpacked = [_unpack(p, [w[n] for n in SMALL]) for p in packed]
    for i, n in enumerate(SMALL):
        res[n] = tuple(u[i] for u in unpacked)

    outs = [loss, dx[None]]
    for which in range(4):
        outs += [res[n][which] for n in ORDER]
    return tuple(outs)
```
